```python
import math
import jax, jax.numpy as jnp
from jax import lax
import numpy as np

D_MODEL = 1024
BATCH = 4
SEQ = 4096
DEPTH = 2
DEC_BATCH = 128
DEC_SEQ = 4
PAST_LEN = 2048
PAGE_SIZE = 128

N_EVEN = (DEPTH + 1) // 2
N_ODD = DEPTH // 2
SB_HEADS = 8
SB_HEAD_DIM = D_MODEL // 16
SB_WIDTH = SB_HEADS * SB_HEAD_DIM
SB_BIAS_INIT = -8.0
Q_BLOCK = 128
LRU_WIDTH = D_MODEL // 2
LRU_BLOCKS = 8
LRU_BLOCK_DIM = LRU_WIDTH // LRU_BLOCKS
LRU_C = 8.0
CONV_WIDTH = 4
DN_HEADS = 8
DN_HEAD_DIM = D_MODEL // 8
DN_WIDTH = DN_HEADS * DN_HEAD_DIM
DN_CHUNK = 64
D_FF = 2 * D_MODEL
EVEN_IN = 3 * SB_WIDTH + 2 * LRU_WIDTH
ODD_IN = 4 * DN_WIDTH + 2 * DN_HEADS
EPS = 1e-6

kernel_name = 'stickbreak_rglru_gdn_macaron_step'


def rms_norm(x, g):
    xf = x.astype(jnp.float32)
    y = xf * lax.rsqrt(jnp.mean(xf * xf, axis=-1, keepdims=True) + EPS)
    return (y * g.astype(jnp.float32)).astype(x.dtype)


def l2_normalize(x):
    xf = x.astype(jnp.float32)
    return xf * lax.rsqrt(jnp.sum(xf * xf, axis=-1, keepdims=True) + EPS)


def swiglu(x, w_in, w_out):
    g, u = jnp.split(x @ w_in, 2, axis=-1)
    return (jax.nn.silu(g) * u) @ w_out


def causal_conv(buf, u, w, b=None):
    xp = jnp.concatenate([buf.astype(u.dtype), u], axis=1)
    T = u.shape[1]
    y = w[0] * xp[:, 0:T]
    for i in range(1, CONV_WIDTH):
        y = y + w[i] * xp[:, i:i + T]
    if b is not None:
        y = y + b
    return y, xp[:, -(CONV_WIDTH - 1):]


def sb_block(q, q_pos, k, v, k_pos, bias):
    z = jnp.einsum('bqhd,bshd->bhqs', q, k, preferred_element_type=jnp.float32) * (SB_HEAD_DIM ** -0.5)
    z = z + bias.astype(jnp.float32)[None, :, None, None]
    mask = k_pos[None, :] < q_pos[:, None]
    log_keep = jnp.where(mask, jax.nn.log_sigmoid(-z), 0.0)
    later = lax.cumsum(log_keep, axis=3, reverse=True) - log_keep
    w = jnp.where(mask, jnp.exp(jax.nn.log_sigmoid(z) + later), 0.0)
    return jnp.einsum('bhqs,bshd->bqhd', w.astype(v.dtype), v)


def sb_attention(q, k, v, q_pos, k_pos, bias):
    B, T, H, Dh = q.shape
    blk = Q_BLOCK if T % Q_BLOCK == 0 else T
    nb = T // blk
    qb = q.reshape(B, nb, blk, H, Dh).transpose(1, 0, 2, 3, 4)
    pb = q_pos.reshape(nb, blk)
    ob = lax.map(lambda a: sb_block(a[0], a[1], k, v, k_pos, bias), (qb, pb))
    return ob.transpose(1, 0, 2, 3, 4).reshape(B, T, H, Dh)


def rg_lru(xc, h0, w_a, b_a, w_i, b_i, lam):
    B, T, W = xc.shape
    xb = xc.reshape(B, T, LRU_BLOCKS, LRU_BLOCK_DIM)
    r = jax.nn.sigmoid((jnp.einsum('btnc,ncd->btnd', xb, w_a).reshape(B, T, W) + b_a).astype(jnp.float32))
    i = jax.nn.sigmoid((jnp.einsum('btnc,ncd->btnd', xb, w_i).reshape(B, T, W) + b_i).astype(jnp.float32))
    log_a = -LRU_C * r * jax.nn.softplus(-lam.astype(jnp.float32))
    a = jnp.exp(log_a)
    b = jnp.sqrt(-jnp.expm1(2.0 * log_a)) * (i * xc.astype(jnp.float32))
    b = b.at[:, 0].add(a[:, 0] * h0.astype(jnp.float32))

    def combine(left, right):
        a1, b1 = left
        a2, b2 = right
        return a1 * a2, a2 * b1 + b2

    _, h = lax.associative_scan(combine, (a, b), axis=1)
    return h, h[:, -1]


def sb_lru_mixer(u, past_k, past_v, conv_buf, h0, w_in, q_gain, k_gain, sb_bias, conv_w, conv_b,
                 w_a, b_a, w_i, b_i, lam, w_out):
    B, T, _ = u.shape
    proj = u @ w_in
    q, k, v, xr, xg = jnp.split(proj, [SB_WIDTH, 2 * SB_WIDTH, 3 * SB_WIDTH, 3 * SB_WIDTH + LRU_WIDTH], axis=-1)
    q = rms_norm(q.reshape(B, T, SB_HEADS, SB_HEAD_DIM), q_gain)
    k = rms_norm(k.reshape(B, T, SB_HEADS, SB_HEAD_DIM), k_gain)
    v = v.reshape(B, T, SB_HEADS, SB_HEAD_DIM)
    past_len = past_k.shape[1]
    k_all = jnp.concatenate([past_k.astype(k.dtype), k], axis=1)
    v_all = jnp.concatenate([past_v.astype(v.dtype), v], axis=1)
    q_pos = past_len + jnp.arange(T)
    k_pos = jnp.arange(past_len + T)
    attn = sb_attention(q, k_all, v_all, q_pos, k_pos, sb_bias).reshape(B, T, SB_WIDTH)
    xc, new_buf = causal_conv(conv_buf, xr, conv_w, conv_b)
    h, h_last = rg_lru(xc, h0, w_a, b_a, w_i, b_i, lam)
    rec = h.astype(u.dtype) * jax.nn.gelu(xg)
    out = jnp.concatenate([attn.astype(u.dtype), rec], axis=-1) @ w_out
    return out, (k, v, new_buf, h_last)


def gated_delta_chunked(q, k, v, beta, g, S0):
    B, T, H, DK = q.shape
    DV = v.shape[-1]
    C = min(DN_CHUNK, T)
    pad = (-T) % C
    f32 = jnp.float32

    def prep(a):
        a = a.astype(f32)
        a = jnp.pad(a, [(0, 0), (0, pad)] + [(0, 0)] * (a.ndim - 2))
        n = a.shape[1] // C
        a = a.reshape((B, n, C) + a.shape[2:])
        return jnp.swapaxes(a, 2, 3)

    qc, kc, vc, bc, gc = prep(q), prep(k), prep(v), prep(beta), prep(g)
    G = jnp.cumsum(gc, axis=-1)
    incl = jnp.tril(jnp.ones((C, C), bool))
    strict = jnp.tril(jnp.ones((C, C), bool), -1)
    decay = jnp.exp(jnp.where(incl, G[..., :, None] - G[..., None, :], -jnp.inf))
    A = jnp.where(strict, jnp.einsum('bnhcd,bnhsd->bnhcs', kc, kc) * decay, 0.0) * bc[..., None]
    rhs = jnp.concatenate([vc * bc[..., None], kc * (bc * jnp.exp(G))[..., None]], axis=-1)
    sol = lax.linalg.triangular_solve(A, rhs, left_side=True, lower=True, unit_diagonal=True)
    Uv, Wk = sol[..., :DV], sol[..., DV:]
    P = jnp.einsum('bnhcd,bnhsd->bnhcs', qc, kc) * decay
    qg = qc * jnp.exp(G)[..., None]
    kg = kc * jnp.exp(G[..., -1:] - G)[..., None]
    gl = jnp.exp(G[..., -1])

    def step(S, xs):
        uv, wk, p, qq, kk, gg = xs
        U = uv - jnp.einsum('bhcd,bhde->bhce', wk, S)
        O = jnp.einsum('bhcd,bhde->bhce', qq, S) + jnp.einsum('bhcs,bhse->bhce', p, U)
        S = gg[..., None, None] * S + jnp.einsum('bhcd,bhce->bhde', kk, U)
        return S, O

    xs = tuple(jnp.moveaxis(a, 1, 0) for a in (Uv, Wk, P, qg, kg, gl))
    S, O = lax.scan(step, S0.astype(f32), xs)
    O = jnp.transpose(O, (1, 0, 3, 2, 4)).reshape(B, -1, H, DV)[:, :T]
    return O, S


def deltanet_mixer(u, conv_buf, S0, w_in, conv_w, A_log, dt_bias, o_gain, w_out):
    B, T, _ = u.shape
    proj = u @ w_in
    qkv, z, b_logit, a_logit = jnp.split(proj, [3 * DN_WIDTH, 4 * DN_WIDTH, 4 * DN_WIDTH + DN_HEADS], axis=-1)
    qkv, new_buf = causal_conv(conv_buf, qkv, conv_w)
    q, k, v = jnp.split(jax.nn.silu(qkv), 3, axis=-1)
    q = l2_normalize(q.reshape(B, T, DN_HEADS, DN_HEAD_DIM)) * (DN_HEAD_DIM ** -0.5)
    k = l2_normalize(k.reshape(B, T, DN_HEADS, DN_HEAD_DIM))
    v = v.reshape(B, T, DN_HEADS, DN_HEAD_DIM)
    beta = jax.nn.sigmoid(b_logit.astype(jnp.float32))
    g = -jnp.exp(A_log.astype(jnp.float32)) * jax.nn.softplus(a_logit.astype(jnp.float32) + dt_bias.astype(jnp.float32))
    o, S = gated_delta_chunked(q, k, v, beta, g, S0)
    o = rms_norm(o, o_gain) * jax.nn.silu(z.reshape(B, T, DN_HEADS, DN_HEAD_DIM).astype(jnp.float32))
    return o.reshape(B, T, DN_WIDTH).astype(u.dtype) @ w_out, (new_buf, S)


def macaron_layer(x, mixer, n1, w1i, w1o, nm, n2, w2i, w2o):
    x = x + 0.5 * swiglu(rms_norm(x, n1), w1i, w1o)
    m, new_state = mixer(rms_norm(x, nm))
    x = x + m
    x = x + 0.5 * swiglu(rms_norm(x, n2), w2i, w2o)
    return x, new_state


def setup_inputs(seed: int = 0) -> dict:
    key = jax.random.key(seed)
    keys = iter(jax.random.split(key, 48))
    f32 = jnp.float32

    def nrm(shape, scale):
        return jax.random.normal(next(keys), shape, f32) * scale

    def gain(shape):
        return 1.0 + nrm(shape, 0.02)

    n_pages = PAST_LEN // PAGE_SIZE
    n_used = DEC_BATCH * n_pages
    n_pool = n_used + max(1, n_used // 4)
    bd = LRU_BLOCK_DIM
    x_prompt = nrm((BATCH, SEQ, D_MODEL), 1.0)
    x_sample = nrm((DEC_BATCH, DEC_SEQ, D_MODEL), 1.0)
    cache_k = nrm((N_EVEN, n_pool, PAGE_SIZE, SB_HEADS, SB_HEAD_DIM), 1.0)
    cache_v = nrm((N_EVEN, n_pool, PAGE_SIZE, SB_HEADS, SB_HEAD_DIM), 1.0)
    state_lru_conv = nrm((N_EVEN, DEC_BATCH, CONV_WIDTH - 1, LRU_WIDTH), 1.0)
    state_lru_h = nrm((N_EVEN, DEC_BATCH, LRU_WIDTH), 0.5)
    state_dn_conv = nrm((N_ODD, DEC_BATCH, CONV_WIDTH - 1, 3 * DN_WIDTH), 1.0)
    state_dn_S = nrm((N_ODD, DEC_BATCH, DN_HEADS, DN_HEAD_DIM, DN_HEAD_DIM), DN_HEAD_DIM ** -0.5)
    page_table = jax.random.permutation(next(keys), n_pool)[:n_used].reshape(DEC_BATCH, n_pages).astype(jnp.int32)
    a_c = jax.random.uniform(next(keys), (N_EVEN, LRU_WIDTH), f32, 0.9, 0.999)
    a_base = a_c ** (1.0 / LRU_C)
    lru_lambda = jnp.log(a_base) - jnp.log1p(-a_base)
    dn_A_log = jnp.log(jax.random.uniform(next(keys), (N_ODD, DN_HEADS), f32, 1.0, 16.0))
    dt0 = jnp.exp(jax.random.uniform(next(keys), (N_ODD, DN_HEADS), f32, math.log(1e-3), math.log(1e-1)))
    dn_dt_bias = dt0 + jnp.log(-jnp.expm1(-dt0))
    return {
        'x_prompt': x_prompt, 'x_sample': x_sample,
        'cache_k': cache_k, 'cache_v': cache_v,
        'state_lru_conv': state_lru_conv, 'state_lru_h': state_lru_h,
        'state_dn_conv': state_dn_conv, 'state_dn_S': state_dn_S,
        'page_table': page_table,
        'norm_ffn1': gain((DEPTH, D_MODEL)),
        'w_ffn1_in': nrm((DEPTH, D_MODEL, 2 * D_FF), D_MODEL ** -0.5),
        'w_ffn1_out': nrm((DEPTH, D_FF, D_MODEL), D_FF ** -0.5),
        'norm_mix': gain((DEPTH, D_MODEL)),
        'norm_ffn2': gain((DEPTH, D_MODEL)),
        'w_ffn2_in': nrm((DEPTH, D_MODEL, 2 * D_FF), D_MODEL ** -0.5),
        'w_ffn2_out': nrm((DEPTH, D_FF, D_MODEL), D_FF ** -0.5),
        'w_in_even': nrm((N_EVEN, D_MODEL, EVEN_IN), D_MODEL ** -0.5),
        'sb_q_gain': gain((N_EVEN, SB_HEAD_DIM)),
        'sb_k_gain': gain((N_EVEN, SB_HEAD_DIM)),
        'sb_bias': SB_BIAS_INIT + nrm((N_EVEN, SB_HEADS), 0.1),
        'lru_conv_w': nrm((N_EVEN, CONV_WIDTH, LRU_WIDTH), CONV_WIDTH ** -0.5),
        'lru_conv_b': nrm((N_EVEN, LRU_WIDTH), 0.01),
        'lru_w_a': nrm((N_EVEN, LRU_BLOCKS, bd, bd), bd ** -0.5),
        'lru_b_a': nrm((N_EVEN, LRU_WIDTH), 0.01),
        'lru_w_i': nrm((N_EVEN, LRU_BLOCKS, bd, bd), bd ** -0.5),
        'lru_b_i': nrm((N_EVEN, LRU_WIDTH), 0.01),
        'lru_lambda': lru_lambda,
        'w_out_even': nrm((N_EVEN, SB_WIDTH + LRU_WIDTH, D_MODEL), (SB_WIDTH + LRU_WIDTH) ** -0.5),
        'w_in_odd': nrm((N_ODD, D_MODEL, ODD_IN), D_MODEL ** -0.5),
        'dn_conv_w': nrm((N_ODD, CONV_WIDTH, 3 * DN_WIDTH), CONV_WIDTH ** -0.5),
        'dn_A_log': dn_A_log,
        'dn_dt_bias': dn_dt_bias,
        'dn_o_gain': gain((N_ODD, DN_HEAD_DIM)),
        'w_out_odd': nrm((N_ODD, DN_WIDTH, D_MODEL), DN_WIDTH ** -0.5),
    }


def reference(x_prompt, x_sample, cache_k, cache_v, state_lru_conv, state_lru_h, state_dn_conv, state_dn_S,
              page_table, norm_ffn1, w_ffn1_in, w_ffn1_out, norm_mix, norm_ffn2, w_ffn2_in, w_ffn2_out,
              w_in_even, sb_q_gain, sb_k_gain, sb_bias, lru_conv_w, lru_conv_b, lru_w_a, lru_b_a, lru_w_i, lru_b_i,
              lru_lambda, w_out_even, w_in_odd, dn_conv_w, dn_A_log, dn_dt_bias, dn_o_gain, w_out_odd):
    dt = x_prompt.dtype
    nb_p = x_prompt.shape[0]
    nb_s = x_sample.shape[0]
    n_pages = page_table.shape[1]
    empty_kv = jnp.zeros((nb_p, 0, SB_HEADS, SB_HEAD_DIM), dt)
    zero_lru_conv = jnp.zeros((nb_p, CONV_WIDTH - 1, LRU_WIDTH), dt)
    zero_lru_h = jnp.zeros((nb_p, LRU_WIDTH), jnp.float32)
    zero_dn_conv = jnp.zeros((nb_p, CONV_WIDTH - 1, 3 * DN_WIDTH), dt)
    zero_dn_S = jnp.zeros((nb_p, DN_HEADS, DN_HEAD_DIM, DN_HEAD_DIM), jnp.float32)
    yp, ys = x_prompt, x_sample
    kp, vp, lcp, lhp, dcp, dsp = [], [], [], [], [], []
    ks, vs, lcs, lhs, dcs, dss = [], [], [], [], [], []
    for l in range(DEPTH):
        ffn = (norm_ffn1[l], w_ffn1_in[l], w_ffn1_out[l], norm_mix[l], norm_ffn2[l], w_ffn2_in[l], w_ffn2_out[l])
        if l % 2 == 0:
            e = l // 2
            wts = (w_in_even[e], sb_q_gain[e], sb_k_gain[e], sb_bias[e], lru_conv_w[e], lru_conv_b[e], lru_w_a[e],
                   lru_b_a[e], lru_w_i[e], lru_b_i[e], lru_lambda[e], w_out_even[e])
            past_k = cache_k[e][page_table].reshape(nb_s, n_pages * PAGE_SIZE, SB_HEADS, SB_HEAD_DIM)
            past_v = cache_v[e][page_table].reshape(nb_s, n_pages * PAGE_SIZE, SB_HEADS, SB_HEAD_DIM)
            yp, (k1, v1, c1, h1) = macaron_layer(
                yp, lambda u: sb_lru_mixer(u, empty_kv, empty_kv, zero_lru_conv, zero_lru_h, *wts), *ffn)
            ys, (k2, v2, c2, h2) = macaron_layer(
                ys, lambda u: sb_lru_mixer(u, past_k, past_v, state_lru_conv[e], state_lru_h[e], *wts), *ffn)
            kp.append(k1); vp.append(v1); lcp.append(c1); lhp.append(h1)
            ks.append(k2); vs.append(v2); lcs.append(c2); lhs.append(h2)
        else:
            o = l // 2
            wts = (w_in_odd[o], dn_conv_w[o], dn_A_log[o], dn_dt_bias[o], dn_o_gain[o], w_out_odd[o])
            yp, (c1, S1) = macaron_layer(yp, lambda u: deltanet_mixer(u, zero_dn_conv, zero_dn_S, *wts), *ffn)
            ys, (c2, S2) = macaron_layer(
                ys, lambda u: deltanet_mixer(u, state_dn_conv[o], state_dn_S[o], *wts), *ffn)
            dcp.append(c1); dsp.append(S1)
            dcs.append(c2); dss.append(S2)
    return (yp, ys,
            jnp.stack(kp), jnp.stack(vp), jnp.stack(lcp), jnp.stack(lhp), jnp.stack(dcp), jnp.stack(dsp),
            jnp.stack(ks), jnp.stack(vs), jnp.stack(lcs), jnp.stack(lhs), jnp.stack(dcs), jnp.stack(dss))
```

```python
import functools
import math

import jax
import jax.numpy as jnp
from jax import lax
from jax.experimental import pallas as pl
from jax.experimental.pallas import tpu as pltpu

F32 = jnp.float32
BF16 = jnp.bfloat16

V7X_LANES = 128
V7X_SUBLANES = 8
V7X_VMEM_LIMIT_BYTES = 56 * 1024 * 1024

D_MODEL = 1024
D_FF = 2 * D_MODEL
SB_HEADS = 8
SB_HEAD_DIM = 64
SB_WIDTH = SB_HEADS * SB_HEAD_DIM
LRU_WIDTH = 512
LRU_BLOCKS = 8
LRU_C = 8.0
CONV_WIDTH = 4
DN_HEADS = 8
DN_HEAD_DIM = 128
DN_WIDTH = DN_HEADS * DN_HEAD_DIM
DN_CHUNK = 64
PAGE_SIZE = 128
EPS = 1e-6

TOKEN_TILE = 512
FF_CHUNK = 512
ATT_TILE = 256
LRU_TIME_TILE = 512


def _params(*sem):
    return pltpu.CompilerParams(dimension_semantics=sem, vmem_limit_bytes=V7X_VMEM_LIMIT_BYTES)


def _resident(shape):
    nd = len(shape)
    return pl.BlockSpec(shape, lambda *_: (0,) * nd, pipeline_mode=pl.Buffered(1))


def _rms_normalize(x, gain):
    ms = jnp.mean(x * x, axis=-1, keepdims=True)
    return x * lax.rsqrt(ms + EPS) * gain


def _softplus(x):
    return jnp.maximum(x, 0.0) + jnp.log(1.0 + jnp.exp(-jnp.abs(x)))


def _split2(x):
    hi = x.astype(BF16)
    lo = (x - hi.astype(F32)).astype(BF16)
    return hi, lo


def _split3(x):
    hi = x.astype(BF16)
    r1 = x - hi.astype(F32)
    mid = r1.astype(BF16)
    lo = (r1 - mid.astype(F32)).astype(BF16)
    return hi, mid, lo


def _dot(a, b):
    return jnp.dot(a, b, preferred_element_type=F32)


def _dot_nt(a, b):
    return lax.dot_general(a, b, (((1,), (1,)), ((), ())), preferred_element_type=F32)


def _dot_tn(a, b):
    return lax.dot_general(a, b, (((0,), (0,)), ((), ())), preferred_element_type=F32)


def _ffn_kernel(x_ref, g_ref, win_ref, wout_ref, o_ref, act_ref):
    x = x_ref[...]
    xn = _rms_normalize(x, g_ref[...]).astype(BF16)
    for c in range(D_FF // FF_CHUNK):
        lo = c * FF_CHUNK
        gate = _dot(xn, win_ref[:, lo:lo + FF_CHUNK])
        up = _dot(xn, win_ref[:, D_FF + lo:D_FF + lo + FF_CHUNK])
        act_ref[:, lo:lo + FF_CHUNK] = (gate * jax.nn.sigmoid(gate) * up).astype(BF16)
    o_ref[...] = x + 0.5 * _dot(act_ref[...], wout_ref[...])


def _ffn(x, gain, w_in, w_out):
    n = x.shape[0]
    tm = min(TOKEN_TILE, n)
    return pl.pallas_call(
        _ffn_kernel,
        grid=(n // tm,),
        in_specs=[
            pl.BlockSpec((tm, D_MODEL), lambda i: (i, 0)),
            _resident((1, D_MODEL)),
            _resident((D_MODEL, 2 * D_FF)),
            _resident((D_FF, D_MODEL)),
        ],
        out_specs=pl.BlockSpec((tm, D_MODEL), lambda i: (i, 0)),
        out_shape=jax.ShapeDtypeStruct((n, D_MODEL), F32),
        scratch_shapes=[pltpu.VMEM((tm, D_FF), BF16)],
        compiler_params=_params("arbitrary"),
        name="ffn",
    )(x, gain.reshape(1, D_MODEL), w_in, w_out)


def _head_rms(x, ones_bd, gain):
    hi, lo = _split2(x * x)
    ms = (_dot(hi, ones_bd) + _dot(lo, ones_bd)) * (1.0 / SB_HEAD_DIM)
    return x * lax.rsqrt(ms + EPS) * gain


def _even_in_kernel(x_ref, g_ref, w_ref, ones_ref, qg_ref, kg_ref,
                    q_ref, k_ref, v_ref, xr_ref, xg_ref):
    xn = _rms_normalize(x_ref[...], g_ref[...]).astype(BF16)
    ones_bd = ones_ref[...]
    q = _dot(xn, w_ref[:, 0:SB_WIDTH])
    q_ref[...] = _head_rms(q, ones_bd, qg_ref[...])
    k = _dot(xn, w_ref[:, SB_WIDTH:2 * SB_WIDTH])
    k_ref[...] = _head_rms(k, ones_bd, kg_ref[...])
    v_ref[...] = _dot(xn, w_ref[:, 2 * SB_WIDTH:3 * SB_WIDTH])
    xr_ref[...] = _dot(xn, w_ref[:, 3 * SB_WIDTH:3 * SB_WIDTH + LRU_WIDTH])
    xg_ref[...] = _dot(xn, w_ref[:, 3 * SB_WIDTH + LRU_WIDTH:])


def _even_in(x, gain, w, ones_bd, q_gain, k_gain):
    n = x.shape[0]
    tm = min(TOKEN_TILE, n)
    width = w.shape[1]
    row = lambda i: (i, 0)
    out = jax.ShapeDtypeStruct((n, SB_WIDTH), F32)
    return pl.pallas_call(
        _even_in_kernel,
        grid=(n // tm,),
        in_specs=[
            pl.BlockSpec((tm, D_MODEL), row),
            _resident((1, D_MODEL)),
            _resident((D_MODEL, width)),
            _resident((SB_WIDTH, SB_WIDTH)),
            _resident((1, SB_WIDTH)),
            _resident((1, SB_WIDTH)),
        ],
        out_specs=[pl.BlockSpec((tm, SB_WIDTH), row)] * 5,
        out_shape=[out] * 5,
        compiler_params=_params("arbitrary"),
        name="even_in",
    )(x, gain.reshape(1, D_MODEL), w, ones_bd, q_gain, k_gain)


def _out_proj_kernel(*refs, n_parts):
    parts, (w_ref, x_ref, o_ref) = refs[:n_parts], refs[n_parts:]
    acc = x_ref[...]
    row = 0
    for p in parts:
        width = p.shape[-1]
        acc = acc + _dot(p[...].astype(BF16), w_ref[row:row + width, :])
        row += width
    o_ref[...] = acc


def _out_proj(parts, w, x):
    n = x.shape[0]
    tm = min(TOKEN_TILE, n)
    row = lambda i: (i, 0)
    return pl.pallas_call(
        functools.partial(_out_proj_kernel, n_parts=len(parts)),
        grid=(n // tm,),
        in_specs=[pl.BlockSpec((tm, p.shape[1]), row) for p in parts]
        + [_resident(w.shape), pl.BlockSpec((tm, D_MODEL), row)],
        out_specs=pl.BlockSpec((tm, D_MODEL), row),
        out_shape=jax.ShapeDtypeStruct((n, D_MODEL), F32),
        compiler_params=_params("arbitrary"),
        name="out_proj",
    )(*parts, w, x)


def _gelu_tanh(x):
    cdf = 0.5 * (1.0 + jnp.tanh(math.sqrt(2.0 / math.pi) * (x + 0.044715 * (x * x * x))))
    return x * cdf


def _lru_coeffs(xc, wa_ref, ba_ref, wi_ref, bi_ref, sp_lam):
    xb = xc.astype(BF16)
    r = jax.nn.sigmoid(_dot(xb, wa_ref[...]) + ba_ref[...])
    i = jax.nn.sigmoid(_dot(xb, wi_ref[...]) + bi_ref[...])
    log_a = -LRU_C * r * sp_lam
    a = jnp.exp(log_a)
    b = jnp.sqrt(-jnp.tanh(log_a) * (a * a + 1.0)) * (i * xc)
    return a, b


def _lru_prompt_kernel(xr_ref, xg_ref, cw_ref, cb_ref, wa_ref, ba_ref, wi_ref, bi_ref, lam_ref,
                       rec_ref, tail_ref, hlast_ref, xp_ref, a_ref, b_ref, h_ref, hc_ref):
    t = pl.program_id(1)
    tt = xr_ref.shape[1]
    pad = V7X_SUBLANES

    @pl.when(t == 0)
    def _():
        xp_ref[0:pad, :] = jnp.zeros((pad, LRU_WIDTH), F32)
        hc_ref[...] = jnp.zeros_like(hc_ref)

    @pl.when(t > 0)
    def _():
        xp_ref[0:pad, :] = xp_ref[tt:tt + pad, :]

    xp_ref[pad:pad + tt, :] = xr_ref[0]
    xc = cw_ref[0:1, :] * xp_ref[pad - 3:pad - 3 + tt, :]
    for j in range(1, CONV_WIDTH):
        xc = xc + cw_ref[j:j + 1, :] * xp_ref[pad - 3 + j:pad - 3 + j + tt, :]
    xc = xc + cb_ref[...]
    a, b = _lru_coeffs(xc, wa_ref, ba_ref, wi_ref, bi_ref, _softplus(-lam_ref[...]))
    a_ref[...] = a
    b_ref[...] = b

    row = lax.broadcasted_iota(jnp.int32, (V7X_SUBLANES, LRU_WIDTH), 0)

    def group(j, h):
        base = pl.multiple_of(j * V7X_SUBLANES, V7X_SUBLANES)
        ag = a_ref[pl.ds(base, V7X_SUBLANES), :]
        bg = b_ref[pl.ds(base, V7X_SUBLANES), :]
        for d in (1, 2, 4):
            keep = row >= d
            a_prev = jnp.where(keep, pltpu.roll(ag, d, axis=0), 1.0)
            b_prev = jnp.where(keep, pltpu.roll(bg, d, axis=0), 0.0)
            bg = ag * b_prev + bg
            ag = ag * a_prev
        hg = ag * h + bg
        h_ref[pl.ds(base, V7X_SUBLANES), :] = hg
        return hg[V7X_SUBLANES - 1:V7X_SUBLANES, :]

    h_end = lax.fori_loop(0, tt // V7X_SUBLANES, group, hc_ref[...])
    hc_ref[...] = h_end
    rec_ref[0] = h_ref[...] * _gelu_tanh(xg_ref[0])

    @pl.when(t == pl.num_programs(1) - 1)
    def _():
        tail_ref[0] = xp_ref[pad + tt - 3:pad + tt, :]
        hlast_ref[0] = h_end


def _lru_prompt(xr, xg, p):
    bsz, t_len, w = xr.shape
    tt = min(LRU_TIME_TILE, t_len)
    tile = pl.BlockSpec((1, tt, w), lambda b, t: (b, t, 0))
    per_b = lambda rows: pl.BlockSpec((1, rows, w), lambda b, t: (b, 0, 0))
    return pl.pallas_call(
        _lru_prompt_kernel,
        grid=(bsz, t_len // tt),
        in_specs=[tile, tile, _resident((CONV_WIDTH, w)), _resident((1, w)), _resident((w, w)),
                  _resident((1, w)), _resident((w, w)), _resident((1, w)), _resident((1, w))],
        out_specs=[tile, per_b(CONV_WIDTH - 1), per_b(1)],
        out_shape=[jax.ShapeDtypeStruct((bsz, t_len, w), F32),
                   jax.ShapeDtypeStruct((bsz, CONV_WIDTH - 1, w), F32),
                   jax.ShapeDtypeStruct((bsz, 1, w), F32)],
        scratch_shapes=[pltpu.VMEM((tt + 2 * V7X_SUBLANES, w), F32), pltpu.VMEM((tt, w), F32),
                        pltpu.VMEM((tt, w), F32), pltpu.VMEM((tt, w), F32), pltpu.VMEM((1, w), F32)],
        compiler_params=_params("arbitrary", "arbitrary"),
        name="lru_prompt",
    )(xr, xg, p["lru_conv_w"], p["lru_conv_b"], p["lru_wa"], p["lru_ba"], p["lru_wi"], p["lru_bi"],
      p["lru_lambda"])


def _lru_sample_kernel(xr_ref, xg_ref, buf_ref, h0_ref, cw_ref, cb_ref, wa_ref, ba_ref, wi_ref, bi_ref,
                       lam_ref, rec_ref, tail_ref, hlast_ref):
    t_len = xr_ref.shape[0]
    xp = [buf_ref[j] for j in range(CONV_WIDTH - 1)] + [xr_ref[j] for j in range(t_len)]
    sp_lam = _softplus(-lam_ref[...])
    h = h0_ref[...]
    for t in range(t_len):
        xc = cw_ref[0:1, :] * xp[t]
        for j in range(1, CONV_WIDTH):
            xc = xc + cw_ref[j:j + 1, :] * xp[t + j]
        xc = xc + cb_ref[...]
        a, b = _lru_coeffs(xc, wa_ref, ba_ref, wi_ref, bi_ref, sp_lam)
        h = a * h + b
        rec_ref[t] = h * _gelu_tanh(xg_ref[t])
    for j in range(CONV_WIDTH - 1):
        tail_ref[j] = xp[t_len + j]
    hlast_ref[...] = h


def _lru_sample(xr, xg, buf, h0, p):
    t_len, bsz, w = xr.shape
    full = lambda shape: pl.BlockSpec(shape, lambda i: (0,) * len(shape))
    return pl.pallas_call(
        _lru_sample_kernel,
        grid=(1,),
        in_specs=[full((t_len, bsz, w)), full((t_len, bsz, w)), full((CONV_WIDTH - 1, bsz, w)), full((bsz, w)),
                  full((CONV_WIDTH, w)), full((1, w)), full((w, w)), full((1, w)), full((w, w)), full((1, w)),
                  full((1, w))],
        out_specs=[full((t_len, bsz, w)), full((CONV_WIDTH - 1, bsz, w)), full((bsz, w))],
        out_shape=[jax.ShapeDtypeStruct((t_len, bsz, w), F32),
                   jax.ShapeDtypeStruct((CONV_WIDTH - 1, bsz, w), F32),
                   jax.ShapeDtypeStruct((bsz, w), F32)],
        compiler_params=_params("arbitrary"),
        name="lru_sample",
    )(xr, xg, buf, h0, p["lru_conv_w"], p["lru_conv_b"], p["lru_wa"], p["lru_ba"], p["lru_wi"], p["lru_bi"],
      p["lru_lambda"])


def _sb_block(qh, kblk, vblk, bias, tri, carry, acc, mask):
    z = _dot_nt(qh, kblk) + bias
    sp = _softplus(z)
    log_beta = z - sp
    if mask is not None:
        sp = jnp.where(mask, sp, 0.0)
    hi, lo = _split2(sp)
    later = _dot(hi, tri) + _dot(lo, tri)
    w = jnp.exp(log_beta - later - carry)
    if mask is not None:
        w = jnp.where(mask, w, 0.0)
    acc = acc + _dot(w.astype(BF16), vblk)
    carry = carry + jnp.sum(sp, axis=-1, keepdims=True)
    return carry, acc


def _sb_prompt_kernel(bias_ref, q_ref, k_ref, v_ref, tri_ref, o_ref):
    hp, qi = pl.program_id(1), pl.program_id(2)
    tq = q_ref.shape[1]
    q = q_ref[0] * (SB_HEAD_DIM ** -0.5)
    lane = lax.broadcasted_iota(jnp.int32, (tq, 2 * SB_HEAD_DIM), 1)
    tri = tri_ref[...]
    causal = (lax.broadcasted_iota(jnp.int32, (tq, tq), 1) < lax.broadcasted_iota(jnp.int32, (tq, tq), 0))

    def kv_block(kb):
        start = pl.multiple_of(kb * tq, tq)
        return k_ref[0, pl.ds(start, tq), :].astype(BF16), v_ref[0, pl.ds(start, tq), :].astype(BF16)

    accs = []
    for h2 in range(2):
        qh = jnp.where((lane < SB_HEAD_DIM) == (h2 == 0), q, 0.0).astype(BF16)
        bias = bias_ref[hp * 2 + h2]
        state = (jnp.zeros((tq, 1), F32), jnp.zeros((tq, 2 * SB_HEAD_DIM), F32))
        state = _sb_block(qh, *kv_block(qi), bias, tri, *state, causal)

        def older(it, st, qh=qh, bias=bias):
            return _sb_block(qh, *kv_block(qi - 1 - it), bias, tri, *st, None)

        accs.append(lax.fori_loop(0, qi, older, state)[1])
    o_ref[0] = jnp.where(lane < SB_HEAD_DIM, accs[0], accs[1])


def _sb_prompt(q, k, v, bias, tri):
    bsz, t_len, width = q.shape
    tq = min(ATT_TILE, t_len)
    pair = 2 * SB_HEAD_DIM
    tile = pl.BlockSpec((1, tq, pair), lambda b, hp, qi: (b, qi, hp))
    seq = pl.BlockSpec((1, t_len, pair), lambda b, hp, qi: (b, 0, hp))
    return pl.pallas_call(
        _sb_prompt_kernel,
        grid=(bsz, width // pair, t_len // tq),
        in_specs=[pl.BlockSpec(memory_space=pltpu.SMEM), tile, seq, seq,
                  pl.BlockSpec((tq, tq), lambda b, hp, qi: (0, 0), pipeline_mode=pl.Buffered(1))],
        out_specs=tile,
        out_shape=jax.ShapeDtypeStruct((bsz, t_len, width), F32),
        compiler_params=_params("arbitrary", "arbitrary", "arbitrary"),
        name="sb_prompt",
    )(bias, q, k, v, tri)


def _sb_sample_kernel(pt_ref, q_ref, kn_ref, vn_ref, bias_ref, tri_ref, *refs, n_pages):
    del pt_ref
    k_pages, v_pages, o_ref = refs[:n_pages], refs[n_pages:2 * n_pages], refs[2 * n_pages]
    t_len, width = q_ref.shape[1], q_ref.shape[2]
    rows = t_len * SB_HEADS
    q = q_ref[0] * (SB_HEAD_DIM ** -0.5)
    qb = jnp.broadcast_to(q[:, None, :], (t_len, SB_HEADS, width)).reshape(rows, width)
    row = lax.broadcasted_iota(jnp.int32, (rows, width), 0)
    lane = lax.broadcasted_iota(jnp.int32, (rows, width), 1)
    own_head = (row & (SB_HEADS - 1)) == (lane >> int(math.log2(SB_HEAD_DIM)))
    qbd = jnp.where(own_head, qb, 0.0).astype(BF16)
    bias, tri = bias_ref[...], tri_ref[...]

    pad = jnp.zeros((PAGE_SIZE - t_len, width), F32)
    k_new = jnp.concatenate([kn_ref[0], pad], axis=0).astype(BF16)
    v_new = jnp.concatenate([vn_ref[0], pad], axis=0).astype(BF16)
    key = lax.broadcasted_iota(jnp.int32, (rows, PAGE_SIZE), 1)
    tok = lax.broadcasted_iota(jnp.int32, (rows, PAGE_SIZE), 0) >> int(math.log2(SB_HEADS))
    state = (jnp.zeros((rows, 1), F32), jnp.zeros((rows, width), F32))
    state = _sb_block(qbd, k_new, v_new, bias, tri, *state, key < tok)
    for p in reversed(range(n_pages)):
        state = _sb_block(qbd, k_pages[p][...].astype(BF16), v_pages[p][...].astype(BF16), bias, tri, *state, None)
    acc = jnp.where(own_head, state[1], 0.0)
    o_ref[0] = jnp.sum(acc.reshape(t_len, SB_HEADS, width), axis=1)


def _sb_sample(q, k_new, v_new, cache_k, cache_v, page_table, bias_rows, tri):
    bsz, t_len, width = q.shape
    n_pages = page_table.shape[1]
    rows = t_len * SB_HEADS
    tok = pl.BlockSpec((1, t_len, width), lambda b, pt: (b, 0, 0))
    const = lambda shape: pl.BlockSpec(shape, lambda b, pt: (0, 0), pipeline_mode=pl.Buffered(1))
    pages = [pl.BlockSpec((None, PAGE_SIZE, width), lambda b, pt, j=j: (pt[b, j], 0, 0)) for j in range(n_pages)]
    grid_spec = pltpu.PrefetchScalarGridSpec(
        num_scalar_prefetch=1,
        grid=(bsz,),
        in_specs=[tok, tok, tok, const((rows, PAGE_SIZE)), const((PAGE_SIZE, PAGE_SIZE))] + pages + pages,
        out_specs=tok,
    )
    return pl.pallas_call(
        functools.partial(_sb_sample_kernel, n_pages=n_pages),
        grid_spec=grid_spec,
        out_shape=jax.ShapeDtypeStruct((bsz, t_len, width), F32),
        compiler_params=_params("arbitrary"),
        name="sb_sample",
    )(page_table, q, k_new, v_new, bias_rows, tri, *([cache_k] * n_pages), *([cache_v] * n_pages))


def _tri_later(n):
    idx = jnp.arange(n)
    return (idx[:, None] > idx[None, :]).astype(BF16)


ODD_COL_CHUNK = 1024
DN_TIME_TILE = 512


def _odd_in_kernel(x_ref, g_ref, w_ref, wba_ref, qkv_ref, z_ref, ba_ref):
    xn = _rms_normalize(x_ref[...], g_ref[...]).astype(BF16)
    for c in range(3 * DN_WIDTH // ODD_COL_CHUNK):
        lo = c * ODD_COL_CHUNK
        qkv_ref[:, lo:lo + ODD_COL_CHUNK] = _dot(xn, w_ref[:, lo:lo + ODD_COL_CHUNK])
    z_ref[...] = _dot(xn, w_ref[:, 3 * DN_WIDTH:])
    ba_ref[...] = _dot(xn, wba_ref[...])


def _odd_in(x, gain, w_qkvz, w_ba):
    n = x.shape[0]
    tm = min(TOKEN_TILE, n)
    row = lambda i: (i, 0)
    return pl.pallas_call(
        _odd_in_kernel,
        grid=(n // tm,),
        in_specs=[pl.BlockSpec((tm, D_MODEL), row), _resident((1, D_MODEL)),
                  _resident(w_qkvz.shape), _resident(w_ba.shape)],
        out_specs=[pl.BlockSpec((tm, 3 * DN_WIDTH), row), pl.BlockSpec((tm, DN_WIDTH), row),
                   pl.BlockSpec((tm, V7X_LANES), row)],
        out_shape=[jax.ShapeDtypeStruct((n, 3 * DN_WIDTH), F32), jax.ShapeDtypeStruct((n, DN_WIDTH), F32),
                   jax.ShapeDtypeStruct((n, V7X_LANES), F32)],
        compiler_params=_params("arbitrary"),
        name="odd_in",
    )(x, gain.reshape(1, D_MODEL), w_qkvz, w_ba)


def _l2_normalize(x):
    return x * lax.rsqrt(jnp.sum(x * x, axis=-1, keepdims=True) + EPS)


def _silu(x):
    return x * jax.nn.sigmoid(x)


def _head_column(x, lane_index):
    lane = lax.broadcasted_iota(jnp.int32, x.shape, 1)
    return jnp.sum(jnp.where(lane == lane_index, x, 0.0), axis=-1, keepdims=True)


def _dn_gates(ba, head, a_log, dt_bias):
    beta = jax.nn.sigmoid(_head_column(ba, head))
    a_logit = _head_column(ba, DN_HEADS + head)
    g = -jnp.exp(jnp.full((1, 1), a_log, F32)) * _softplus(a_logit + dt_bias)
    return beta, g


def _dn_prompt_kernel(alog_ref, dtb_ref, q_ref, k_ref, v_ref, z_ref, ba_ref, cwq_ref, cwk_ref, cwv_ref,
                      gain_ref, low_ref, uo_ref, o_ref, s_ref, xq_ref, xk_ref, xv_ref, state_ref):
    head, t = pl.program_id(1), pl.program_id(2)
    tt = q_ref.shape[1]
    pad = V7X_SUBLANES
    cs = DN_CHUNK

    @pl.when(t == 0)
    def _():
        state_ref[...] = jnp.zeros_like(state_ref)

    def conv_silu(x_ref, xp_ref, cw_ref):
        @pl.when(t == 0)
        def _():
            xp_ref[0:pad, :] = jnp.zeros((pad, DN_HEAD_DIM), F32)

        @pl.when(t > 0)
        def _():
            xp_ref[0:pad, :] = xp_ref[tt:tt + pad, :]

        xp_ref[pad:pad + tt, :] = x_ref[0]
        y = cw_ref[0:1, :] * xp_ref[pad - 3:pad - 3 + tt, :]
        for j in range(1, CONV_WIDTH):
            y = y + cw_ref[j:j + 1, :] * xp_ref[pad - 3 + j:pad - 3 + j + tt, :]
        return _silu(y)

    q_all = _l2_normalize(conv_silu(q_ref, xq_ref, cwq_ref)) * (DN_HEAD_DIM ** -0.5)
    k_all = _l2_normalize(conv_silu(k_ref, xk_ref, cwk_ref))
    v_all = conv_silu(v_ref, xv_ref, cwv_ref)
    beta_all, g_all = _dn_gates(ba_ref[0], head, alog_ref[head], dtb_ref[head])

    low, uo = low_ref[...], uo_ref[...]
    row = lax.broadcasted_iota(jnp.int32, (cs, cs), 0)
    col = lax.broadcasted_iota(jnp.int32, (cs, cs), 1)
    state = state_ref[...]
    for c in range(tt // cs):
        rows = slice(c * cs, (c + 1) * cs)
        qc, kc, vc, beta, g = q_all[rows], k_all[rows], v_all[rows], beta_all[rows], g_all[rows]
        parts = _split3(g * uo)
        gfull = _dot(low, parts[0]) + _dot(low, parts[1]) + _dot(low, parts[2])
        decay = jnp.exp(gfull[:, 0:cs])
        g_col = gfull[:, 2 * cs:]
        e_g = jnp.exp(g_col)
        g_last = g_col[cs - 1:cs, :]
        kb = kc.astype(BF16)
        a_mat = jnp.where(col < row, _dot_nt(kb, kb) * decay, 0.0) * beta
        p_mat = jnp.where(col <= row, _dot_nt(qc.astype(BF16), kb) * decay, 0.0)
        rhs = jnp.concatenate([vc * beta, kc * (beta * e_g)], axis=1)
        t_off, x = -a_mat, a_mat
        for _ in range(int(math.log2(cs)) - 1):
            xb = x.astype(BF16)
            x = _dot(xb, xb)
            t_off = t_off + x + _dot(t_off.astype(BF16), x.astype(BF16))
        sol = rhs + _dot(t_off.astype(BF16), rhs.astype(BF16))
        uv, wk = sol[:, 0:DN_HEAD_DIM], sol[:, DN_HEAD_DIM:]
        qg = qc * e_g
        kg = kc * jnp.exp(g_last - g_col)
        sb = state.astype(BF16)
        u = uv - _dot(wk.astype(BF16), sb)
        ub = u.astype(BF16)
        o = _dot(qg.astype(BF16), sb) + _dot(p_mat.astype(BF16), ub)
        state = jnp.exp(g_last) * state + _dot_tn(kg.astype(BF16), ub)
        o_ref[0, rows, :] = _rms_normalize(o, gain_ref[...]) * _silu(z_ref[0, rows, :])
    state_ref[...] = state

    @pl.when(t == pl.num_programs(2) - 1)
    def _():
        s_ref[0, 0] = state


def _dn_prompt(qkv, z, ba, p):
    bsz, t_len, _ = qkv.shape
    tt = min(DN_TIME_TILE, t_len)
    dh = DN_HEAD_DIM
    lane_blk = lambda off: pl.BlockSpec((1, tt, dh), lambda b, h, t: (b, t, off + h))
    cw_blk = lambda off: pl.BlockSpec((CONV_WIDTH, dh), lambda b, h, t: (0, off + h))
    const = lambda shape: pl.BlockSpec(shape, lambda b, h, t: (0,) * len(shape), pipeline_mode=pl.Buffered(1))
    smem = pl.BlockSpec(memory_space=pltpu.SMEM)
    return pl.pallas_call(
        _dn_prompt_kernel,
        grid=(bsz, DN_HEADS, t_len // tt),
        in_specs=[smem, smem, lane_blk(0), lane_blk(DN_HEADS), lane_blk(2 * DN_HEADS), lane_blk(0),
                  pl.BlockSpec((1, tt, V7X_LANES), lambda b, h, t: (b, t, 0)),
                  cw_blk(0), cw_blk(DN_HEADS), cw_blk(2 * DN_HEADS),
                  const((1, dh)), const((DN_CHUNK, DN_CHUNK)), const((DN_CHUNK, 4 * DN_CHUNK))],
        out_specs=[lane_blk(0), pl.BlockSpec((1, 1, dh, dh), lambda b, h, t: (b, h, 0, 0))],
        out_shape=[jax.ShapeDtypeStruct((bsz, t_len, DN_WIDTH), F32),
                   jax.ShapeDtypeStruct((bsz, DN_HEADS, dh, dh), F32)],
        scratch_shapes=[pltpu.VMEM((tt + 2 * V7X_SUBLANES, dh), F32)] * 3 + [pltpu.VMEM((dh, dh), F32)],
        compiler_params=_params("arbitrary", "arbitrary", "arbitrary"),
        name="dn_prompt",
    )(p["dn_A_log"], p["dn_dt_bias"], qkv, qkv, qkv, z, ba, p["dn_conv_w"], p["dn_conv_w"], p["dn_conv_w"],
      p["dn_o_gain"], p["dn_low"], p["dn_uo"])


def _dn_sample_kernel(alog_ref, dtb_ref, qkv_ref, buf_ref, z_ref, ba_ref, s0_ref, cw_ref, gain_ref,
                      o_ref, tail_ref, s_ref):
    t_len = qkv_ref.shape[1]
    dh = DN_HEAD_DIM
    xp = jnp.concatenate([buf_ref[0], qkv_ref[0]], axis=0)
    y = cw_ref[0:1, :] * xp[0:t_len]
    for j in range(1, CONV_WIDTH):
        y = y + cw_ref[j:j + 1, :] * xp[j:j + t_len]
    y = _silu(y)
    tail_ref[0] = xp[t_len:t_len + CONV_WIDTH - 1]
    ba = ba_ref[0]
    for h in range(DN_HEADS):
        q = _l2_normalize(y[:, h * dh:(h + 1) * dh]) * (dh ** -0.5)
        k = _l2_normalize(y[:, DN_WIDTH + h * dh:DN_WIDTH + (h + 1) * dh])
        v = y[:, 2 * DN_WIDTH + h * dh:2 * DN_WIDTH + (h + 1) * dh]
        beta, g = _dn_gates(ba, h, alog_ref[h], dtb_ref[h])
        cols = jnp.concatenate([q, k], axis=0).T
        state = s0_ref[0, h]
        outs = []
        for t in range(t_len):
            q_col, k_col = cols[:, t:t + 1], cols[:, t_len + t:t_len + t + 1]
            state = jnp.exp(g[t:t + 1, :]) * state
            u = beta[t:t + 1, :] * (v[t:t + 1, :] - jnp.sum(k_col * state, axis=0, keepdims=True))
            state = state + k_col * u
            outs.append(jnp.sum(q_col * state, axis=0, keepdims=True))
        s_ref[0, h] = state
        o = jnp.concatenate(outs, axis=0)
        o_ref[0, :, h * dh:(h + 1) * dh] = _rms_normalize(o, gain_ref[...]) * _silu(z_ref[0, :, h * dh:(h + 1) * dh])


def _dn_sample(qkv, buf, z, ba, s0, p):
    bsz, t_len, _ = qkv.shape
    dh = DN_HEAD_DIM
    per_b = lambda *shape: pl.BlockSpec((1,) + shape, lambda b: (b,) + (0,) * len(shape))
    const = lambda shape: pl.BlockSpec(shape, lambda b: (0,) * len(shape), pipeline_mode=pl.Buffered(1))
    smem = pl.BlockSpec(memory_space=pltpu.SMEM)
    return pl.pallas_call(
        _dn_sample_kernel,
        grid=(bsz,),
        in_specs=[smem, smem, per_b(t_len, 3 * DN_WIDTH), per_b(CONV_WIDTH - 1, 3 * DN_WIDTH), per_b(t_len, DN_WIDTH),
                  per_b(t_len, V7X_LANES), per_b(DN_HEADS, dh, dh), const((CONV_WIDTH, 3 * DN_WIDTH)), const((1, dh))],
        out_specs=[per_b(t_len, DN_WIDTH), per_b(CONV_WIDTH - 1, 3 * DN_WIDTH), per_b(DN_HEADS, dh, dh)],
        out_shape=[jax.ShapeDtypeStruct((bsz, t_len, DN_WIDTH), F32),
                   jax.ShapeDtypeStruct((bsz, CONV_WIDTH - 1, 3 * DN_WIDTH), F32),
                   jax.ShapeDtypeStruct((bsz, DN_HEADS, dh, dh), F32)],
        compiler_params=_params("arbitrary"),
        name="dn_sample",
    )(p["dn_A_log"], p["dn_dt_bias"], qkv, buf, z, ba, s0, p["dn_conv_w"], p["dn_o_gain"])


def _block_diag(w):
    n, c, d = w.shape
    eye = jnp.eye(n, dtype=w.dtype)
    return (eye[:, None, :, None] * w[:, :, None, :]).reshape(n * c, n * d)


def _prepare_weights(d):
    p = {}
    for name in ("w_ffn1_in", "w_ffn1_out", "w_ffn2_in", "w_ffn2_out"):
        p[name] = d[name].astype(BF16)
    p["w_in_even"] = d["w_in_even"][0].astype(BF16)
    p["w_out_even"] = d["w_out_even"][0].astype(BF16)
    p["ones_bd"] = _block_diag(jnp.ones((SB_HEADS, SB_HEAD_DIM, SB_HEAD_DIM), BF16))
    p["q_gain"] = jnp.tile(d["sb_q_gain"][0], SB_HEADS).reshape(1, SB_WIDTH)
    p["k_gain"] = jnp.tile(d["sb_k_gain"][0], SB_HEADS).reshape(1, SB_WIDTH)
    p["lru_conv_w"] = d["lru_conv_w"][0]
    p["lru_conv_b"] = d["lru_conv_b"][0].reshape(1, LRU_WIDTH)
    p["lru_wa"] = _block_diag(d["lru_w_a"][0]).astype(BF16)
    p["lru_wi"] = _block_diag(d["lru_w_i"][0]).astype(BF16)
    p["lru_ba"] = d["lru_b_a"][0].reshape(1, LRU_WIDTH)
    p["lru_bi"] = d["lru_b_i"][0].reshape(1, LRU_WIDTH)
    p["lru_lambda"] = d["lru_lambda"][0].reshape(1, LRU_WIDTH)
    w_odd = d["w_in_odd"][0]
    p["w_qkvz"] = w_odd[:, :4 * DN_WIDTH].astype(BF16)
    p["w_ba"] = jnp.pad(w_odd[:, 4 * DN_WIDTH:], ((0, 0), (0, V7X_LANES - 2 * DN_HEADS))).astype(BF16)
    p["w_out_odd"] = d["w_out_odd"][0].astype(BF16)
    p["dn_conv_w"] = d["dn_conv_w"][0]
    p["dn_A_log"] = d["dn_A_log"][0]
    p["dn_dt_bias"] = d["dn_dt_bias"][0]
    p["dn_o_gain"] = d["dn_o_gain"][0].reshape(1, DN_HEAD_DIM)
    idx = jnp.arange(DN_CHUNK)
    p["dn_low"] = (idx[None, :] <= idx[:, None]).astype(BF16)
    later = (idx[:, None] > idx[None, :]).astype(F32)
    p["dn_uo"] = jnp.concatenate([later, jnp.zeros((DN_CHUNK, DN_CHUNK), F32), jnp.ones((DN_CHUNK, 2 * DN_CHUNK), F32)], axis=1)
    return p


def _even_layer_mixer(x, group, d, p):
    kind, bsz, t_len = group
    q, k, v, xr, xg = _even_in(x, d["norm_mix"][0], p["w_in_even"], p["ones_bd"], p["q_gain"], p["k_gain"])
    seq = lambda a: a.reshape(bsz, t_len, a.shape[-1])
    if kind == "prompt":
        attn = _sb_prompt(seq(q), seq(k), seq(v), d["sb_bias"][0], _tri_later(min(ATT_TILE, t_len)))
        rec, tail, h_last = _lru_prompt(seq(xr), seq(xg), p)
        h_last = h_last[:, 0]
    else:
        bias_rows = jnp.broadcast_to(jnp.tile(d["sb_bias"][0], t_len)[:, None], (t_len * SB_HEADS, PAGE_SIZE))
        cache_k = d["cache_k"][0].reshape(-1, PAGE_SIZE, SB_WIDTH)
        cache_v = d["cache_v"][0].reshape(-1, PAGE_SIZE, SB_WIDTH)
        attn = _sb_sample(seq(q), seq(k), seq(v), cache_k, cache_v, d["page_table"], bias_rows, _tri_later(PAGE_SIZE))
        tm = lambda a: jnp.swapaxes(a, 0, 1)
        rec, tail, h_last = _lru_sample(tm(seq(xr)), tm(seq(xg)), tm(d["state_lru_conv"][0]), d["state_lru_h"][0], p)
        rec, tail = tm(rec), tm(tail)
    x = _out_proj([attn.reshape(-1, SB_WIDTH), rec.reshape(-1, LRU_WIDTH)], p["w_out_even"], x)
    heads = lambda a: a.reshape(bsz, t_len, SB_HEADS, SB_HEAD_DIM)
    return x, (heads(k), heads(v), tail, h_last)


def _odd_layer_mixer(x, group, d, p):
    kind, bsz, t_len = group
    qkv, z, ba = _odd_in(x, d["norm_mix"][1], p["w_qkvz"], p["w_ba"])
    seq = lambda a: a.reshape(bsz, t_len, a.shape[-1])
    if kind == "prompt":
        o, state = _dn_prompt(seq(qkv), seq(z), seq(ba), p)
        tail = seq(qkv)[:, t_len - (CONV_WIDTH - 1):, :]
    else:
        o, tail, state = _dn_sample(seq(qkv), d["state_dn_conv"][0], seq(z), seq(ba), d["state_dn_S"][0], p)
    x = _out_proj([o.reshape(-1, DN_WIDTH)], p["w_out_odd"], x)
    return x, (tail, state)


def kernel(x_prompt, x_sample, cache_k, cache_v, state_lru_conv, state_lru_h, state_dn_conv, state_dn_S, page_table, norm_ffn1, w_ffn1_in, w_ffn1_out, norm_mix, norm_ffn2, w_ffn2_in, w_ffn2_out, w_in_even, sb_q_gain, sb_k_gain, sb_bias, lru_conv_w, lru_conv_b, lru_w_a, lru_b_a, lru_w_i, lru_b_i, lru_lambda, w_out_even, w_in_odd, dn_conv_w, dn_A_log, dn_dt_bias, dn_o_gain, w_out_odd):
    d = dict(locals())
    assert norm_ffn1.shape[0] == 2 and w_in_even.shape[0] == 1 and w_in_odd.shape[0] == 1, "two-layer trunk only"
    p = _prepare_weights(d)
    results = []
    for kind, x in (("prompt", x_prompt), ("sample", x_sample)):
        bsz, t_len, _ = x.shape
        group = (kind, bsz, t_len)
        x = x.reshape(bsz * t_len, D_MODEL)
        x = _ffn(x, norm_ffn1[0], p["w_ffn1_in"][0], p["w_ffn1_out"][0])
        x, even_state = _even_layer_mixer(x, group, d, p)
        x = _ffn(x, norm_ffn2[0], p["w_ffn2_in"][0], p["w_ffn2_out"][0])
        x = _ffn(x, norm_ffn1[1], p["w_ffn1_in"][1], p["w_ffn1_out"][1])
        x, odd_state = _odd_layer_mixer(x, group, d, p)
        x = _ffn(x, norm_ffn2[1], p["w_ffn2_in"][1], p["w_ffn2_out"][1])
        results.append((x.reshape(bsz, t_len, D_MODEL),) + tuple(s[None] for s in even_state + odd_state))
    (yp, *prompt_state), (ys, *sample_state) = results
    return (yp, ys, *prompt_state, *sample_state)
```

```python
import functools
import math

import jax
import jax.numpy as jnp
from jax import lax
from jax.experimental import pallas as pl
from jax.experimental.pallas import tpu as pltpu

F32 = jnp.float32
BF16 = jnp.bfloat16

V7X_LANES = 128
V7X_SUBLANES = 8
V7X_VMEM_LIMIT_BYTES = 56 * 1024 * 1024

D_MODEL = 1024
D_FF = 2 * D_MODEL
SB_HEADS = 8
SB_HEAD_DIM = 64
SB_WIDTH = SB_HEADS * SB_HEAD_DIM
LRU_WIDTH = 512
LRU_BLOCKS = 8
LRU_C = 8.0
CONV_WIDTH = 4
DN_HEADS = 8
DN_HEAD_DIM = 128
DN_WIDTH = DN_HEADS * DN_HEAD_DIM
DN_CHUNK = 64
PAGE_SIZE = 128
EPS = 1e-6

TOKEN_TILE = 512
FF_CHUNK = 512
ATT_TILE = 256
LRU_TIME_TILE = 512


def _params(*sem):
    return pltpu.CompilerParams(dimension_semantics=sem, vmem_limit_bytes=V7X_VMEM_LIMIT_BYTES)


def _resident(shape):
    nd = len(shape)
    return pl.BlockSpec(shape, lambda *_: (0,) * nd, pipeline_mode=pl.Buffered(1))


def _rms_normalize(x, gain):
    ms = jnp.mean(x * x, axis=-1, keepdims=True)
    return x * lax.rsqrt(ms + EPS) * gain


def _softplus(x):
    return jnp.maximum(x, 0.0) + jnp.log(1.0 + jnp.exp(-jnp.abs(x)))


def _split2(x):
    hi = x.astype(BF16)
    lo = (x - hi.astype(F32)).astype(BF16)
    return hi, lo


def _split3(x):
    hi = x.astype(BF16)
    r1 = x - hi.astype(F32)
    mid = r1.astype(BF16)
    lo = (r1 - mid.astype(F32)).astype(BF16)
    return hi, mid, lo


def _dot(a, b):
    return jnp.dot(a, b, preferred_element_type=F32)


def _dot_nt(a, b):
    return lax.dot_general(a, b, (((1,), (1,)), ((), ())), preferred_element_type=F32)


def _dot_tn(a, b):
    return lax.dot_general(a, b, (((0,), (0,)), ((), ())), preferred_element_type=F32)


def _ffn_kernel(x_ref, g_ref, win_ref, wout_ref, o_ref, act_ref):
    x = x_ref[...]
    xn = _rms_normalize(x, g_ref[...]).astype(BF16)
    for c in range(D_FF // FF_CHUNK):
        lo = c * FF_CHUNK
        gate = _dot(xn, win_ref[:, lo:lo + FF_CHUNK])
        up = _dot(xn, win_ref[:, D_FF + lo:D_FF + lo + FF_CHUNK])
        act_ref[:, lo:lo + FF_CHUNK] = (gate * jax.nn.sigmoid(gate) * up).astype(BF16)
    o_ref[...] = x + 0.5 * _dot(act_ref[...], wout_ref[...])


def _ffn(x, gain, w_in, w_out):
    n = x.shape[0]
    tm = min(TOKEN_TILE, n)
    return pl.pallas_call(
        _ffn_kernel,
        grid=(n // tm,),
        in_specs=[
            pl.BlockSpec((tm, D_MODEL), lambda i: (i, 0)),
            _resident((1, D_MODEL)),
            _resident((D_MODEL, 2 * D_FF)),
            _resident((D_FF, D_MODEL)),
        ],
        out_specs=pl.BlockSpec((tm, D_MODEL), lambda i: (i, 0)),
        out_shape=jax.ShapeDtypeStruct((n, D_MODEL), F32),
        scratch_shapes=[pltpu.VMEM((tm, D_FF), BF16)],
        compiler_params=_params("arbitrary"),
        name="ffn",
    )(x, gain.reshape(1, D_MODEL), w_in, w_out)


def _head_rms(x, ones_bd, gain):
    hi, lo = _split2(x * x)
    ms = (_dot(hi, ones_bd) + _dot(lo, ones_bd)) * (1.0 / SB_HEAD_DIM)
    return x * lax.rsqrt(ms + EPS) * gain


def _even_in_kernel(x_ref, g_ref, w_ref, ones_ref, qg_ref, kg_ref,
                    q_ref, k_ref, v_ref, xr_ref, xg_ref):
    xn = _rms_normalize(x_ref[...], g_ref[...]).astype(BF16)
    ones_bd = ones_ref[...]
    q = _dot(xn, w_ref[:, 0:SB_WIDTH])
    q_ref[...] = _head_rms(q, ones_bd, qg_ref[...])
    k = _dot(xn, w_ref[:, SB_WIDTH:2 * SB_WIDTH])
    k_ref[...] = _head_rms(k, ones_bd, kg_ref[...])
    v_ref[...] = _dot(xn, w_ref[:, 2 * SB_WIDTH:3 * SB_WIDTH])
    xr_ref[...] = _dot(xn, w_ref[:, 3 * SB_WIDTH:3 * SB_WIDTH + LRU_WIDTH])
    xg_ref[...] = _dot(xn, w_ref[:, 3 * SB_WIDTH + LRU_WIDTH:])


def _even_in(x, gain, w, ones_bd, q_gain, k_gain):
    n = x.shape[0]
    tm = min(TOKEN_TILE, n)
    width = w.shape[1]
    row = lambda i: (i, 0)
    out = jax.ShapeDtypeStruct((n, SB_WIDTH), F32)
    return pl.pallas_call(
        _even_in_kernel,
        grid=(n // tm,),
        in_specs=[
            pl.BlockSpec((tm, D_MODEL), row),
            _resident((1, D_MODEL)),
            _resident((D_MODEL, width)),
            _resident((SB_WIDTH, SB_WIDTH)),
            _resident((1, SB_WIDTH)),
            _resident((1, SB_WIDTH)),
        ],
        out_specs=[pl.BlockSpec((tm, SB_WIDTH), row)] * 5,
        out_shape=[out] * 5,
        compiler_params=_params("arbitrary"),
        name="even_in",
    )(x, gain.reshape(1, D_MODEL), w, ones_bd, q_gain, k_gain)


def _out_proj_kernel(*refs, n_parts):
    parts, (w_ref, x_ref, o_ref) = refs[:n_parts], refs[n_parts:]
    acc = x_ref[...]
    row = 0
    for p in parts:
        width = p.shape[-1]
        acc = acc + _dot(p[...].astype(BF16), w_ref[row:row + width, :])
        row += width
    o_ref[...] = acc


def _out_proj(parts, w, x):
    n = x.shape[0]
    tm = min(TOKEN_TILE, n)
    row = lambda i: (i, 0)
    return pl.pallas_call(
        functools.partial(_out_proj_kernel, n_parts=len(parts)),
        grid=(n // tm,),
        in_specs=[pl.BlockSpec((tm, p.shape[1]), row) for p in parts]
        + [_resident(w.shape), pl.BlockSpec((tm, D_MODEL), row)],
        out_specs=pl.BlockSpec((tm, D_MODEL), row),
        out_shape=jax.ShapeDtypeStruct((n, D_MODEL), F32),
        compiler_params=_params("arbitrary"),
        name="out_proj",
    )(*parts, w, x)


def _gelu_tanh(x):
    cdf = 0.5 * (1.0 + jnp.tanh(math.sqrt(2.0 / math.pi) * (x + 0.044715 * (x * x * x))))
    return x * cdf


def _lru_coeffs(xc, wa_ref, ba_ref, wi_ref, bi_ref, sp_lam):
    xb = xc.astype(BF16)
    r = jax.nn.sigmoid(_dot(xb, wa_ref[...]) + ba_ref[...])
    i = jax.nn.sigmoid(_dot(xb, wi_ref[...]) + bi_ref[...])
    log_a = -LRU_C * r * sp_lam
    a = jnp.exp(log_a)
    b = jnp.sqrt(-jnp.tanh(log_a) * (a * a + 1.0)) * (i * xc)
    return a, b


def _lru_prompt_kernel(xr_ref, xg_ref, cw_ref, cb_ref, wa_ref, ba_ref, wi_ref, bi_ref, lam_ref,
                       rec_ref, tail_ref, hlast_ref, xp_ref, a_ref, b_ref, h_ref, hc_ref):
    t = pl.program_id(1)
    tt = xr_ref.shape[1]
    pad = V7X_SUBLANES

    @pl.when(t == 0)
    def _():
        xp_ref[0:pad, :] = jnp.zeros((pad, LRU_WIDTH), F32)
        hc_ref[...] = jnp.zeros_like(hc_ref)

    @pl.when(t > 0)
    def _():
        xp_ref[0:pad, :] = xp_ref[tt:tt + pad, :]

    xp_ref[pad:pad + tt, :] = xr_ref[0]
    xc = cw_ref[0:1, :] * xp_ref[pad - 3:pad - 3 + tt, :]
    for j in range(1, CONV_WIDTH):
        xc = xc + cw_ref[j:j + 1, :] * xp_ref[pad - 3 + j:pad - 3 + j + tt, :]
    xc = xc + cb_ref[...]
    a, b = _lru_coeffs(xc, wa_ref, ba_ref, wi_ref, bi_ref, _softplus(-lam_ref[...]))
    a_ref[...] = a
    b_ref[...] = b

    row = lax.broadcasted_iota(jnp.int32, (V7X_SUBLANES, LRU_WIDTH), 0)

    def group(j, h):
        base = pl.multiple_of(j * V7X_SUBLANES, V7X_SUBLANES)
        ag = a_ref[pl.ds(base, V7X_SUBLANES), :]
        bg = b_ref[pl.ds(base, V7X_SUBLANES), :]
        for d in (1, 2, 4):
            keep = row >= d
            a_prev = jnp.where(keep, pltpu.roll(ag, d, axis=0), 1.0)
            b_prev = jnp.where(keep, pltpu.roll(bg, d, axis=0), 0.0)
            bg = ag * b_prev + bg
            ag = ag * a_prev
        hg = ag * h + bg
        h_ref[pl.ds(base, V7X_SUBLANES), :] = hg
        return hg[V7X_SUBLANES - 1:V7X_SUBLANES, :]

    h_end = lax.fori_loop(0, tt // V7X_SUBLANES, group, hc_ref[...])
    hc_ref[...] = h_end
    rec_ref[0] = h_ref[...] * _gelu_tanh(xg_ref[0])

    @pl.when(t == pl.num_programs(1) - 1)
    def _():
        tail_ref[0] = xp_ref[pad + tt - 3:pad + tt, :]
        hlast_ref[0] = h_end


def _lru_prompt(xr, xg, p):
    bsz, t_len, w = xr.shape
    tt = min(LRU_TIME_TILE, t_len)
    tile = pl.BlockSpec((1, tt, w), lambda b, t: (b, t, 0))
    per_b = lambda rows: pl.BlockSpec((1, rows, w), lambda b, t: (b, 0, 0))
    return pl.pallas_call(
        _lru_prompt_kernel,
        grid=(bsz, t_len // tt),
        in_specs=[tile, tile, _resident((CONV_WIDTH, w)), _resident((1, w)), _resident((w, w)),
                  _resident((1, w)), _resident((w, w)), _resident((1, w)), _resident((1, w))],
        out_specs=[tile, per_b(CONV_WIDTH - 1), per_b(1)],
        out_shape=[jax.ShapeDtypeStruct((bsz, t_len, w), F32),
                   jax.ShapeDtypeStruct((bsz, CONV_WIDTH - 1, w), F32),
                   jax.ShapeDtypeStruct((bsz, 1, w), F32)],
        scratch_shapes=[pltpu.VMEM((tt + 2 * V7X_SUBLANES, w), F32), pltpu.VMEM((tt, w), F32),
                        pltpu.VMEM((tt, w), F32), pltpu.VMEM((tt, w), F32), pltpu.VMEM((1, w), F32)],
        compiler_params=_params("arbitrary", "arbitrary"),
        name="lru_prompt",
    )(xr, xg, p["lru_conv_w"], p["lru_conv_b"], p["lru_wa"], p["lru_ba"], p["lru_wi"], p["lru_bi"],
      p["lru_lambda"])


def _lru_sample_kernel(xr_ref, xg_ref, buf_ref, h0_ref, cw_ref, cb_ref, wa_ref, ba_ref, wi_ref, bi_ref,
                       lam_ref, rec_ref, tail_ref, hlast_ref):
    t_len = xr_ref.shape[0]
    xp = [buf_ref[j] for j in range(CONV_WIDTH - 1)] + [xr_ref[j] for j in range(t_len)]
    sp_lam = _softplus(-lam_ref[...])
    h = h0_ref[...]
    for t in range(t_len):
        xc = cw_ref[0:1, :] * xp[t]
        for j in range(1, CONV_WIDTH):
            xc = xc + cw_ref[j:j + 1, :] * xp[t + j]
        xc = xc + cb_ref[...]
        a, b = _lru_coeffs(xc, wa_ref, ba_ref, wi_ref, bi_ref, sp_lam)
        h = a * h + b
        rec_ref[t] = h * _gelu_tanh(xg_ref[t])
    for j in range(CONV_WIDTH - 1):
        tail_ref[j] = xp[t_len + j]
    hlast_ref[...] = h


def _lru_sample(xr, xg, buf, h0, p):
    t_len, bsz, w = xr.shape
    full = lambda shape: pl.BlockSpec(shape, lambda i: (0,) * len(shape))
    return pl.pallas_call(
        _lru_sample_kernel,
        grid=(1,),
        in_specs=[full((t_len, bsz, w)), full((t_len, bsz, w)), full((CONV_WIDTH - 1, bsz, w)), full((bsz, w)),
                  full((CONV_WIDTH, w)), full((1, w)), full((w, w)), full((1, w)), full((w, w)), full((1, w)),
                  full((1, w))],
        out_specs=[full((t_len, bsz, w)), full((CONV_WIDTH - 1, bsz, w)), full((bsz, w))],
        out_shape=[jax.ShapeDtypeStruct((t_len, bsz, w), F32),
                   jax.ShapeDtypeStruct((CONV_WIDTH - 1, bsz, w), F32),
                   jax.ShapeDtypeStruct((bsz, w), F32)],
        compiler_params=_params("arbitrary"),
        name="lru_sample",
    )(xr, xg, buf, h0, p["lru_conv_w"], p["lru_conv_b"], p["lru_wa"], p["lru_ba"], p["lru_wi"], p["lru_bi"],
      p["lru_lambda"])


def _sb_chains(scores, masks, bias, tri, carry):
    rows = scores[0].shape[0]
    zs, parts = [], []
    for s, m in zip(scores, masks):
        z = s + bias
        sp = _softplus(z)
        if m is not None:
            sp = jnp.where(m, sp, 0.0)
        zs.append(z)
        parts.extend(_split2(sp))
    cum = _dot(jnp.concatenate(parts, axis=0), tri)
    ws = []
    for i, (z, m) in enumerate(zip(zs, masks)):
        c = cum[2 * i * rows:(2 * i + 1) * rows] + cum[(2 * i + 1) * rows:(2 * i + 2) * rows]
        w = jnp.exp(z - c - carry)
        if m is not None:
            w = jnp.where(m, w, 0.0)
        ws.append(w.astype(BF16))
        carry = carry + c[:, 0:1]
    return carry, ws


def _sb_prompt_kernel(bias_ref, q_ref, k_ref, v_ref, tri_ref, o_ref):
    hp, qi = pl.program_id(1), pl.program_id(2)
    tq = q_ref.shape[1]
    q = q_ref[0] * (SB_HEAD_DIM ** -0.5)
    lane = lax.broadcasted_iota(jnp.int32, (tq, 2 * SB_HEAD_DIM), 1)
    q2 = jnp.concatenate([jnp.where(lane < SB_HEAD_DIM, q, 0.0), jnp.where(lane < SB_HEAD_DIM, 0.0, q)], axis=0)
    q2 = q2.astype(BF16)
    first_head = lax.broadcasted_iota(jnp.int32, (2 * tq, 1), 0) < tq
    bias = jnp.where(first_head, bias_ref[hp * 2], bias_ref[hp * 2 + 1])
    tri = tri_ref[...]
    n_old = lax.shift_right_logical(qi, 1)

    def pair_step(j, masks, carry, acc):
        start = pl.multiple_of(j * 2 * tq, 2 * tq)
        kb = k_ref[0, pl.ds(start, 2 * tq), :].astype(BF16)
        vb = v_ref[0, pl.ds(start, 2 * tq), :].astype(BF16)
        s = _dot_nt(q2, kb)
        carry, ws = _sb_chains([s[:, tq:], s[:, :tq]], masks, bias, tri, carry)
        return carry, acc + _dot(jnp.concatenate([ws[1], ws[0]], axis=1), vb)

    q_pos = qi * tq + (lax.broadcasted_iota(jnp.int32, (2 * tq, tq), 0) & (tq - 1))
    k_pos = n_old * 2 * tq + lax.broadcasted_iota(jnp.int32, (2 * tq, tq), 1)
    state = (jnp.zeros((2 * tq, 1), F32), jnp.zeros((2 * tq, 2 * SB_HEAD_DIM), F32))
    state = pair_step(n_old, [k_pos + tq < q_pos, k_pos < q_pos], *state)
    state = lax.fori_loop(0, n_old, lambda it, st: pair_step(n_old - 1 - it, [None, None], *st), state)
    acc = state[1]
    o_ref[0] = jnp.where(lane < SB_HEAD_DIM, acc[:tq], acc[tq:])


def _sb_prompt(q, k, v, bias, tri):
    bsz, t_len, width = q.shape
    tq = ATT_TILE
    assert t_len % (2 * tq) == 0 and tq & (tq - 1) == 0
    pair = 2 * SB_HEAD_DIM
    tile = pl.BlockSpec((1, tq, pair), lambda b, hp, qi: (b, qi, hp))
    seq = pl.BlockSpec((1, t_len, pair), lambda b, hp, qi: (b, 0, hp))
    return pl.pallas_call(
        _sb_prompt_kernel,
        grid=(bsz, width // pair, t_len // tq),
        in_specs=[pl.BlockSpec(memory_space=pltpu.SMEM), tile, seq, seq,
                  pl.BlockSpec((tq, tq), lambda b, hp, qi: (0, 0), pipeline_mode=pl.Buffered(1))],
        out_specs=tile,
        out_shape=jax.ShapeDtypeStruct((bsz, t_len, width), F32),
        compiler_params=_params("arbitrary", "arbitrary", "arbitrary"),
        name="sb_prompt",
    )(bias, q, k, v, tri)


def _sb_sample_kernel(pt_ref, q_ref, kn_ref, vn_ref, bias_ref, tri_ref, *refs, n_pages):
    del pt_ref
    k_pages, v_pages, o_ref = refs[:n_pages], refs[n_pages:2 * n_pages], refs[2 * n_pages]
    t_len, width = q_ref.shape[1], q_ref.shape[2]
    rows = t_len * SB_HEADS
    q = q_ref[0] * (SB_HEAD_DIM ** -0.5)
    qb = jnp.broadcast_to(q[:, None, :], (t_len, SB_HEADS, width)).reshape(rows, width)
    row = lax.broadcasted_iota(jnp.int32, (rows, width), 0)
    lane = lax.broadcasted_iota(jnp.int32, (rows, width), 1)
    own_head = (row & (SB_HEADS - 1)) == (lane >> int(math.log2(SB_HEAD_DIM)))
    qbd = jnp.where(own_head, qb, 0.0).astype(BF16)
    page = lambda ref: ref[...].reshape(width, PAGE_SIZE).astype(BF16)

    pad = jnp.zeros((PAGE_SIZE - t_len, width), F32)
    k_new = jnp.concatenate([kn_ref[0], pad], axis=0).astype(BF16)
    v_new = jnp.concatenate([vn_ref[0], pad], axis=0).astype(BF16)
    key = lax.broadcasted_iota(jnp.int32, (rows, PAGE_SIZE), 1)
    tok = lax.broadcasted_iota(jnp.int32, (rows, PAGE_SIZE), 0) >> int(math.log2(SB_HEADS))
    order = list(reversed(range(n_pages)))
    scores = [_dot_nt(qbd, k_new)] + [_dot(qbd, page(k_pages[p])) for p in order]
    masks = [key < tok] + [None] * n_pages
    _, ws = _sb_chains(scores, masks, bias_ref[...], tri_ref[...], jnp.zeros((rows, 1), F32))
    acc = _dot(ws[0], v_new)
    for w, p in zip(ws[1:], order):
        acc = acc + _dot_nt(w, page(v_pages[p]))
    acc = jnp.where(own_head, acc, 0.0)
    o_ref[0] = jnp.sum(acc.reshape(t_len, SB_HEADS, width), axis=1)


def _sb_sample(q, k_new, v_new, cache_k, cache_v, page_table, bias_rows, tri):
    bsz, t_len, width = q.shape
    n_pages = page_table.shape[1]
    rows = t_len * SB_HEADS
    tok = pl.BlockSpec((1, t_len, width), lambda b, pt: (b, 0, 0))
    const = lambda shape: pl.BlockSpec(shape, lambda b, pt: (0, 0), pipeline_mode=pl.Buffered(1))
    pages = [pl.BlockSpec((None, SB_HEADS, SB_HEAD_DIM, PAGE_SIZE), lambda b, pt, j=j: (pt[b, j], 0, 0, 0))
             for j in range(n_pages)]
    grid_spec = pltpu.PrefetchScalarGridSpec(
        num_scalar_prefetch=1,
        grid=(bsz,),
        in_specs=[tok, tok, tok, const((rows, PAGE_SIZE)), const((PAGE_SIZE, PAGE_SIZE))] + pages + pages,
        out_specs=tok,
    )
    return pl.pallas_call(
        functools.partial(_sb_sample_kernel, n_pages=n_pages),
        grid_spec=grid_spec,
        out_shape=jax.ShapeDtypeStruct((bsz, t_len, width), F32),
        compiler_params=_params("arbitrary"),
        name="sb_sample",
    )(page_table, q, k_new, v_new, bias_rows, tri, *([cache_k] * n_pages), *([cache_v] * n_pages))


def _tri_suffix(n):
    idx = jnp.arange(n)
    return (idx[:, None] >= idx[None, :]).astype(BF16)


ODD_COL_CHUNK = 1024
DN_TIME_TILE = 512
DN_GROUP = 2


def _odd_in_kernel(x_ref, g_ref, w_ref, wba_ref, qkv_ref, z_ref, ba_ref):
    xn = _rms_normalize(x_ref[...], g_ref[...]).astype(BF16)
    for c in range(3 * DN_WIDTH // ODD_COL_CHUNK):
        lo = c * ODD_COL_CHUNK
        qkv_ref[:, lo:lo + ODD_COL_CHUNK] = _dot(xn, w_ref[:, lo:lo + ODD_COL_CHUNK])
    z_ref[...] = _dot(xn, w_ref[:, 3 * DN_WIDTH:])
    ba_ref[...] = _dot(xn, wba_ref[...])


def _odd_in(x, gain, w_qkvz, w_ba):
    n = x.shape[0]
    tm = min(TOKEN_TILE, n)
    row = lambda i: (i, 0)
    return pl.pallas_call(
        _odd_in_kernel,
        grid=(n // tm,),
        in_specs=[pl.BlockSpec((tm, D_MODEL), row), _resident((1, D_MODEL)),
                  _resident(w_qkvz.shape), _resident(w_ba.shape)],
        out_specs=[pl.BlockSpec((tm, 3 * DN_WIDTH), row), pl.BlockSpec((tm, DN_WIDTH), row),
                   pl.BlockSpec((tm, V7X_LANES), row)],
        out_shape=[jax.ShapeDtypeStruct((n, 3 * DN_WIDTH), F32), jax.ShapeDtypeStruct((n, DN_WIDTH), F32),
                   jax.ShapeDtypeStruct((n, V7X_LANES), F32)],
        compiler_params=_params("arbitrary"),
        name="odd_in",
    )(x, gain.reshape(1, D_MODEL), w_qkvz, w_ba)


def _l2_normalize(x):
    return x * lax.rsqrt(jnp.sum(x * x, axis=-1, keepdims=True) + EPS)


def _silu(x):
    return x * jax.nn.sigmoid(x)


def _head_column(x, lane_index):
    lane = lax.broadcasted_iota(jnp.int32, x.shape, 1)
    return jnp.sum(jnp.where(lane == lane_index, x, 0.0), axis=-1, keepdims=True)


def _dn_gates(ba, head, a_log, dt_bias):
    beta = jax.nn.sigmoid(_head_column(ba, head))
    a_logit = _head_column(ba, DN_HEADS + head)
    g = -jnp.exp(jnp.full((1, 1), a_log, F32)) * _softplus(a_logit + dt_bias)
    return beta, g


def _dn_prompt_kernel(alog_ref, dtb_ref, q_ref, k_ref, v_ref, z_ref, ba_ref, cwq_ref, cwk_ref, cwv_ref,
                      gain_ref, low_ref, uo_ref, o_ref, s_ref, xq_ref, xk_ref, xv_ref, state_ref):
    grp, t = pl.program_id(1), pl.program_id(2)
    tt, gw = q_ref.shape[1], q_ref.shape[2]
    dh, cs, pad = DN_HEAD_DIM, DN_CHUNK, V7X_SUBLANES
    n_heads, n_chunks = gw // dh, tt // cs

    @pl.when(t == 0)
    def _():
        state_ref[...] = jnp.zeros_like(state_ref)

    def conv_silu(x_ref, xp_ref, cw_ref):
        @pl.when(t == 0)
        def _():
            xp_ref[0:pad, :] = jnp.zeros((pad, gw), F32)

        @pl.when(t > 0)
        def _():
            xp_ref[0:pad, :] = xp_ref[tt:tt + pad, :]

        xp_ref[pad:pad + tt, :] = x_ref[0]
        y = cw_ref[0:1, :] * xp_ref[pad - 3:pad - 3 + tt, :]
        for j in range(1, CONV_WIDTH):
            y = y + cw_ref[j:j + 1, :] * xp_ref[pad - 3 + j:pad - 3 + j + tt, :]
        return _silu(y)

    q_c, k_c, v_c = conv_silu(q_ref, xq_ref, cwq_ref), conv_silu(k_ref, xk_ref, cwk_ref), conv_silu(v_ref, xv_ref, cwv_ref)
    ba = ba_ref[0]
    per_head = {name: [] for name in ("q", "k", "v", "beta", "g")}
    for h in range(n_heads):
        lanes = slice(h * dh, (h + 1) * dh)
        head = grp * n_heads + h
        beta_h, g_h = _dn_gates(ba, head, alog_ref[head], dtb_ref[head])
        per_head["q"].append(_l2_normalize(q_c[:, lanes]) * (dh ** -0.5))
        per_head["k"].append(_l2_normalize(k_c[:, lanes]))
        per_head["v"].append(v_c[:, lanes])
        per_head["beta"].append(beta_h)
        per_head["g"].append(g_h)
    nb = n_heads * n_chunks
    stack = lambda name: jnp.concatenate(per_head[name], axis=0).reshape(nb, cs, -1)
    qc, kc, vc, beta, g = stack("q"), stack("k"), stack("v"), stack("beta"), stack("g")

    bmm = lambda a, b: lax.dot_general(a, b, (((2,), (1,)), ((0,), (0,))), preferred_element_type=F32)
    bmm_nt = lambda a, b: lax.dot_general(a, b, (((2,), (2,)), ((0,), (0,))), preferred_element_type=F32)
    low = jnp.broadcast_to(low_ref[...][None], (nb, cs, cs))
    row = lax.broadcasted_iota(jnp.int32, (nb, cs, cs), 1)
    col = lax.broadcasted_iota(jnp.int32, (nb, cs, cs), 2)
    parts = _split3(g * uo_ref[...][None])
    gfull = bmm(low, parts[0]) + bmm(low, parts[1]) + bmm(low, parts[2])
    decay = jnp.exp(gfull[:, :, 0:cs])
    g_col = gfull[:, :, 2 * cs:]
    e_g = jnp.exp(g_col)
    g_last = g_col[:, cs - 1:cs, :]
    kb = kc.astype(BF16)
    a_mat = jnp.where(col < row, bmm_nt(kb, kb) * decay, 0.0) * beta
    p_mat = jnp.where(col <= row, bmm_nt(qc.astype(BF16), kb) * decay, 0.0).astype(BF16)
    rhs = jnp.concatenate([vc * beta, kc * (beta * e_g)], axis=2)
    t_off, x = -a_mat, a_mat
    for _ in range(int(math.log2(cs)) - 1):
        xb = x.astype(BF16)
        x = bmm(xb, xb)
        t_off = t_off + x + bmm(t_off.astype(BF16), x.astype(BF16))
    sol = rhs + bmm(t_off.astype(BF16), rhs.astype(BF16))
    uv, wk = sol[:, :, 0:dh], sol[:, :, dh:].astype(BF16)
    qg = (qc * e_g).astype(BF16)
    kg = (kc * jnp.exp(g_last - g_col)).astype(BF16)
    g_end = jnp.exp(g_last)

    states = [state_ref[h] for h in range(n_heads)]
    for c in range(n_chunks):
        rows = slice(c * cs, (c + 1) * cs)
        for h in range(n_heads):
            i = h * n_chunks + c
            sb = states[h].astype(BF16)
            ws = _dot(jnp.concatenate([wk[i], qg[i]], axis=0), sb)
            ub = (uv[i] - ws[0:cs]).astype(BF16)
            o = ws[cs:] + _dot(p_mat[i], ub)
            states[h] = g_end[i] * states[h] + _dot_tn(kg[i], ub)
            lanes = slice(h * dh, (h + 1) * dh)
            o_ref[0, rows, lanes] = _rms_normalize(o, gain_ref[...]) * _silu(z_ref[0, rows, lanes])
    for h in range(n_heads):
        state_ref[h] = states[h]

    @pl.when(t == pl.num_programs(2) - 1)
    def _():
        for h in range(n_heads):
            s_ref[0, h] = states[h]


def _dn_prompt(qkv, z, ba, p):
    bsz, t_len, _ = qkv.shape
    tt = min(DN_TIME_TILE, t_len)
    dh = DN_HEAD_DIM
    gw = DN_GROUP * dh
    groups = DN_HEADS // DN_GROUP
    lane_blk = lambda off: pl.BlockSpec((1, tt, gw), lambda b, g, t: (b, t, off + g))
    cw_blk = lambda off: pl.BlockSpec((CONV_WIDTH, gw), lambda b, g, t: (0, off + g))
    const = lambda shape: pl.BlockSpec(shape, lambda b, g, t: (0,) * len(shape), pipeline_mode=pl.Buffered(1))
    smem = pl.BlockSpec(memory_space=pltpu.SMEM)
    return pl.pallas_call(
        _dn_prompt_kernel,
        grid=(bsz, groups, t_len // tt),
        in_specs=[smem, smem, lane_blk(0), lane_blk(groups), lane_blk(2 * groups), lane_blk(0),
                  pl.BlockSpec((1, tt, V7X_LANES), lambda b, g, t: (b, t, 0)),
                  cw_blk(0), cw_blk(groups), cw_blk(2 * groups),
                  const((1, dh)), const((DN_CHUNK, DN_CHUNK)), const((DN_CHUNK, 4 * DN_CHUNK))],
        out_specs=[lane_blk(0), pl.BlockSpec((1, DN_GROUP, dh, dh), lambda b, g, t: (b, g, 0, 0))],
        out_shape=[jax.ShapeDtypeStruct((bsz, t_len, DN_WIDTH), F32),
                   jax.ShapeDtypeStruct((bsz, DN_HEADS, dh, dh), F32)],
        scratch_shapes=[pltpu.VMEM((tt + 2 * V7X_SUBLANES, gw), F32)] * 3 + [pltpu.VMEM((DN_GROUP, dh, dh), F32)],
        compiler_params=_params("arbitrary", "arbitrary", "arbitrary"),
        name="dn_prompt",
    )(p["dn_A_log"], p["dn_dt_bias"], qkv, qkv, qkv, z, ba, p["dn_conv_w"], p["dn_conv_w"], p["dn_conv_w"],
      p["dn_o_gain"], p["dn_low"], p["dn_uo"])


def _dn_sample_kernel(alog_ref, dtb_ref, qkv_ref, buf_ref, z_ref, ba_ref, s0_ref, cw_ref, gain_ref,
                      o_ref, tail_ref, s_ref):
    t_len = qkv_ref.shape[1]
    dh = DN_HEAD_DIM
    xp = jnp.concatenate([buf_ref[0], qkv_ref[0]], axis=0)
    y = cw_ref[0:1, :] * xp[0:t_len]
    for j in range(1, CONV_WIDTH):
        y = y + cw_ref[j:j + 1, :] * xp[j:j + t_len]
    y = _silu(y)
    tail_ref[0] = xp[t_len:t_len + CONV_WIDTH - 1]
    ba = ba_ref[0]
    for h in range(DN_HEADS):
        q = _l2_normalize(y[:, h * dh:(h + 1) * dh]) * (dh ** -0.5)
        k = _l2_normalize(y[:, DN_WIDTH + h * dh:DN_WIDTH + (h + 1) * dh])
        v = y[:, 2 * DN_WIDTH + h * dh:2 * DN_WIDTH + (h + 1) * dh]
        beta, g = _dn_gates(ba, h, alog_ref[h], dtb_ref[h])
        cols = jnp.concatenate([q, k], axis=0).T
        state = s0_ref[0, h]
        outs = []
        for t in range(t_len):
            q_col, k_col = cols[:, t:t + 1], cols[:, t_len + t:t_len + t + 1]
            state = jnp.exp(g[t:t + 1, :]) * state
            u = beta[t:t + 1, :] * (v[t:t + 1, :] - jnp.sum(k_col * state, axis=0, keepdims=True))
            state = state + k_col * u
            outs.append(jnp.sum(q_col * state, axis=0, keepdims=True))
        s_ref[0, h] = state
        o = jnp.concatenate(outs, axis=0)
        o_ref[0, :, h * dh:(h + 1) * dh] = _rms_normalize(o, gain_ref[...]) * _silu(z_ref[0, :, h * dh:(h + 1) * dh])


def _dn_sample(qkv, buf, z, ba, s0, p):
    bsz, t_len, _ = qkv.shape
    dh = DN_HEAD_DIM
    per_b = lambda *shape: pl.BlockSpec((1,) + shape, lambda b: (b,) + (0,) * len(shape))
    const = lambda shape: pl.BlockSpec(shape, lambda b: (0,) * len(shape), pipeline_mode=pl.Buffered(1))
    smem = pl.BlockSpec(memory_space=pltpu.SMEM)
    return pl.pallas_call(
        _dn_sample_kernel,
        grid=(bsz,),
        in_specs=[smem, smem, per_b(t_len, 3 * DN_WIDTH), per_b(CONV_WIDTH - 1, 3 * DN_WIDTH), per_b(t_len, DN_WIDTH),
                  per_b(t_len, V7X_LANES), per_b(DN_HEADS, dh, dh), const((CONV_WIDTH, 3 * DN_WIDTH)), const((1, dh))],
        out_specs=[per_b(t_len, DN_WIDTH), per_b(CONV_WIDTH - 1, 3 * DN_WIDTH), per_b(DN_HEADS, dh, dh)],
        out_shape=[jax.ShapeDtypeStruct((bsz, t_len, DN_WIDTH), F32),
                   jax.ShapeDtypeStruct((bsz, CONV_WIDTH - 1, 3 * DN_WIDTH), F32),
                   jax.ShapeDtypeStruct((bsz, DN_HEADS, dh, dh), F32)],
        compiler_params=_params("arbitrary"),
        name="dn_sample",
    )(p["dn_A_log"], p["dn_dt_bias"], qkv, buf, z, ba, s0, p["dn_conv_w"], p["dn_o_gain"])


def _block_diag(w):
    n, c, d = w.shape
    eye = jnp.eye(n, dtype=w.dtype)
    return (eye[:, None, :, None] * w[:, :, None, :]).reshape(n * c, n * d)


def _prepare_weights(d):
    p = {}
    for name in ("w_ffn1_in", "w_ffn1_out", "w_ffn2_in", "w_ffn2_out"):
        p[name] = d[name].astype(BF16)
    p["w_in_even"] = d["w_in_even"][0].astype(BF16)
    p["w_out_even"] = d["w_out_even"][0].astype(BF16)
    p["ones_bd"] = _block_diag(jnp.ones((SB_HEADS, SB_HEAD_DIM, SB_HEAD_DIM), BF16))
    p["q_gain"] = jnp.tile(d["sb_q_gain"][0], SB_HEADS).reshape(1, SB_WIDTH)
    p["k_gain"] = jnp.tile(d["sb_k_gain"][0], SB_HEADS).reshape(1, SB_WIDTH)
    p["lru_conv_w"] = d["lru_conv_w"][0]
    p["lru_conv_b"] = d["lru_conv_b"][0].reshape(1, LRU_WIDTH)
    p["lru_wa"] = _block_diag(d["lru_w_a"][0]).astype(BF16)
    p["lru_wi"] = _block_diag(d["lru_w_i"][0]).astype(BF16)
    p["lru_ba"] = d["lru_b_a"][0].reshape(1, LRU_WIDTH)
    p["lru_bi"] = d["lru_b_i"][0].reshape(1, LRU_WIDTH)
    p["lru_lambda"] = d["lru_lambda"][0].reshape(1, LRU_WIDTH)
    w_odd = d["w_in_odd"][0]
    p["w_qkvz"] = w_odd[:, :4 * DN_WIDTH].astype(BF16)
    p["w_ba"] = jnp.pad(w_odd[:, 4 * DN_WIDTH:], ((0, 0), (0, V7X_LANES - 2 * DN_HEADS))).astype(BF16)
    p["w_out_odd"] = d["w_out_odd"][0].astype(BF16)
    p["dn_conv_w"] = d["dn_conv_w"][0]
    p["dn_A_log"] = d["dn_A_log"][0]
    p["dn_dt_bias"] = d["dn_dt_bias"][0]
    p["dn_o_gain"] = d["dn_o_gain"][0].reshape(1, DN_HEAD_DIM)
    idx = jnp.arange(DN_CHUNK)
    p["dn_low"] = (idx[None, :] <= idx[:, None]).astype(BF16)
    later = (idx[:, None] > idx[None, :]).astype(F32)
    p["dn_uo"] = jnp.concatenate([later, jnp.zeros((DN_CHUNK, DN_CHUNK), F32), jnp.ones((DN_CHUNK, 2 * DN_CHUNK), F32)], axis=1)
    return p


def _even_layer_mixer(x, group, d, p):
    kind, bsz, t_len = group
    q, k, v, xr, xg = _even_in(x, d["norm_mix"][0], p["w_in_even"], p["ones_bd"], p["q_gain"], p["k_gain"])
    seq = lambda a: a.reshape(bsz, t_len, a.shape[-1])
    if kind == "prompt":
        attn = _sb_prompt(seq(q), seq(k), seq(v), d["sb_bias"][0], _tri_suffix(ATT_TILE))
        rec, tail, h_last = _lru_prompt(seq(xr), seq(xg), p)
        h_last = h_last[:, 0]
    else:
        bias_rows = jnp.broadcast_to(jnp.tile(d["sb_bias"][0], t_len)[:, None], (t_len * SB_HEADS, PAGE_SIZE))
        cache_k = jnp.transpose(d["cache_k"][0], (0, 2, 3, 1))
        cache_v = jnp.transpose(d["cache_v"][0], (0, 2, 3, 1))
        attn = _sb_sample(seq(q), seq(k), seq(v), cache_k, cache_v, d["page_table"], bias_rows, _tri_suffix(PAGE_SIZE))
        tm = lambda a: jnp.swapaxes(a, 0, 1)
        rec, tail, h_last = _lru_sample(tm(seq(xr)), tm(seq(xg)), tm(d["state_lru_conv"][0]), d["state_lru_h"][0], p)
        rec, tail = tm(rec), tm(tail)
    x = _out_proj([attn.reshape(-1, SB_WIDTH), rec.reshape(-1, LRU_WIDTH)], p["w_out_even"], x)
    heads = lambda a: a.reshape(bsz, t_len, SB_HEADS, SB_HEAD_DIM)
    return x, (heads(k), heads(v), tail, h_last)


def _odd_layer_mixer(x, group, d, p):
    kind, bsz, t_len = group
    qkv, z, ba = _odd_in(x, d["norm_mix"][1], p["w_qkvz"], p["w_ba"])
    seq = lambda a: a.reshape(bsz, t_len, a.shape[-1])
    if kind == "prompt":
        o, state = _dn_prompt(seq(qkv), seq(z), seq(ba), p)
        tail = seq(qkv)[:, t_len - (CONV_WIDTH - 1):, :]
    else:
        o, tail, state = _dn_sample(seq(qkv), d["state_dn_conv"][0], seq(z), seq(ba), d["state_dn_S"][0], p)
    x = _out_proj([o.reshape(-1, DN_WIDTH)], p["w_out_odd"], x)
    return x, (tail, state)


def kernel(x_prompt, x_sample, cache_k, cache_v, state_lru_conv, state_lru_h, state_dn_conv, state_dn_S, page_table, norm_ffn1, w_ffn1_in, w_ffn1_out, norm_mix, norm_ffn2, w_ffn2_in, w_ffn2_out, w_in_even, sb_q_gain, sb_k_gain, sb_bias, lru_conv_w, lru_conv_b, lru_w_a, lru_b_a, lru_w_i, lru_b_i, lru_lambda, w_out_even, w_in_odd, dn_conv_w, dn_A_log, dn_dt_bias, dn_o_gain, w_out_odd):
    d = dict(locals())
    assert norm_ffn1.shape[0] == 2 and w_in_even.shape[0] == 1 and w_in_odd.shape[0] == 1, "two-layer trunk only"
    p = _prepare_weights(d)
    results = []
    for kind, x in (("prompt", x_prompt), ("sample", x_sample)):
        bsz, t_len, _ = x.shape
        group = (kind, bsz, t_len)
        x = x.reshape(bsz * t_len, D_MODEL)
        x = _ffn(x, norm_ffn1[0], p["w_ffn1_in"][0], p["w_ffn1_out"][0])
        x, even_state = _even_layer_mixer(x, group, d, p)
        x = _ffn(x, norm_ffn2[0], p["w_ffn2_in"][0], p["w_ffn2_out"][0])
        x = _ffn(x, norm_ffn1[1], p["w_ffn1_in"][1], p["w_ffn1_out"][1])
        x, odd_state = _odd_layer_mixer(x, group, d, p)
        x = _ffn(x, norm_ffn2[1], p["w_ffn2_in"][1], p["w_ffn2_out"][1])
        results.append((x.reshape(bsz, t_len, D_MODEL),) + tuple(s[None] for s in even_state + odd_state))
    (yp, *prompt_state), (ys, *sample_state) = results
    return (yp, ys, *prompt_state, *sample_state)
```

```python
import functools
import math

import jax
import jax.numpy as jnp
from jax import lax
from jax.experimental import pallas as pl
from jax.experimental.pallas import tpu as pltpu

F32 = jnp.float32
BF16 = jnp.bfloat16

V7X_LANES = 128
V7X_SUBLANES = 8
V7X_VMEM_LIMIT_BYTES = 56 * 1024 * 1024

D_MODEL = 1024
D_FF = 2 * D_MODEL
SB_HEADS = 8
SB_HEAD_DIM = 64
SB_WIDTH = SB_HEADS * SB_HEAD_DIM
LRU_WIDTH = 512
LRU_BLOCKS = 8
LRU_C = 8.0
CONV_WIDTH = 4
DN_HEADS = 8
DN_HEAD_DIM = 128
DN_WIDTH = DN_HEADS * DN_HEAD_DIM
DN_CHUNK = 64
PAGE_SIZE = 128
EPS = 1e-6

TOKEN_TILE = 512
FF_CHUNK = 512
ATT_TILE = 256
LRU_TIME_TILE = 512


def _params(*sem):
    return pltpu.CompilerParams(dimension_semantics=sem, vmem_limit_bytes=V7X_VMEM_LIMIT_BYTES)


def _resident(shape):
    nd = len(shape)
    return pl.BlockSpec(shape, lambda *_: (0,) * nd, pipeline_mode=pl.Buffered(1))


def _rms_normalize(x, gain):
    ms = jnp.mean(x * x, axis=-1, keepdims=True)
    return x * lax.rsqrt(ms + EPS) * gain


LOG2E = 1.0 / math.log(2.0)


def _softplus(x):
    return jnp.maximum(x, 0.0) + jnp.log(1.0 + jnp.exp2(jnp.abs(x) * (-LOG2E)))


def _split2(x):
    hi = x.astype(BF16)
    lo = (x - hi.astype(F32)).astype(BF16)
    return hi, lo


def _split3(x):
    hi = x.astype(BF16)
    r1 = x - hi.astype(F32)
    mid = r1.astype(BF16)
    lo = (r1 - mid.astype(F32)).astype(BF16)
    return hi, mid, lo


def _dot(a, b):
    return jnp.dot(a, b, preferred_element_type=F32)


def _dot_nt(a, b):
    return lax.dot_general(a, b, (((1,), (1,)), ((), ())), preferred_element_type=F32)


def _dot_tn(a, b):
    return lax.dot_general(a, b, (((0,), (0,)), ((), ())), preferred_element_type=F32)


def _ffn_kernel(x_ref, g_ref, win_ref, wout_ref, o_ref, act_ref):
    x = x_ref[...]
    xn = _rms_normalize(x, g_ref[...]).astype(BF16)
    for c in range(D_FF // FF_CHUNK):
        lo = c * FF_CHUNK
        gate = _dot(xn, win_ref[:, lo:lo + FF_CHUNK])
        up = _dot(xn, win_ref[:, D_FF + lo:D_FF + lo + FF_CHUNK])
        act_ref[:, lo:lo + FF_CHUNK] = (gate * jax.nn.sigmoid(gate) * up).astype(BF16)
    o_ref[...] = x + 0.5 * _dot(act_ref[...], wout_ref[...])


def _ffn(x, gain, w_in, w_out):
    n = x.shape[0]
    tm = min(TOKEN_TILE, n)
    return pl.pallas_call(
        _ffn_kernel,
        grid=(n // tm,),
        in_specs=[
            pl.BlockSpec((tm, D_MODEL), lambda i: (i, 0)),
            _resident((1, D_MODEL)),
            _resident((D_MODEL, 2 * D_FF)),
            _resident((D_FF, D_MODEL)),
        ],
        out_specs=pl.BlockSpec((tm, D_MODEL), lambda i: (i, 0)),
        out_shape=jax.ShapeDtypeStruct((n, D_MODEL), F32),
        scratch_shapes=[pltpu.VMEM((tm, D_FF), BF16)],
        compiler_params=_params("arbitrary"),
        name="ffn",
    )(x, gain.reshape(1, D_MODEL), w_in, w_out)


def _head_rms(x, ones_bd, gain):
    hi, lo = _split2(x * x)
    ms = (_dot(hi, ones_bd) + _dot(lo, ones_bd)) * (1.0 / SB_HEAD_DIM)
    return x * lax.rsqrt(ms + EPS) * gain


def _even_in_kernel(x_ref, g_ref, w_ref, ones_ref, qg_ref, kg_ref,
                    q_ref, k_ref, v_ref, xr_ref, xg_ref):
    xn = _rms_normalize(x_ref[...], g_ref[...]).astype(BF16)
    ones_bd = ones_ref[...]
    q = _dot(xn, w_ref[:, 0:SB_WIDTH])
    q_ref[...] = _head_rms(q, ones_bd, qg_ref[...])
    k = _dot(xn, w_ref[:, SB_WIDTH:2 * SB_WIDTH])
    k_ref[...] = _head_rms(k, ones_bd, kg_ref[...])
    v_ref[...] = _dot(xn, w_ref[:, 2 * SB_WIDTH:3 * SB_WIDTH])
    xr_ref[...] = _dot(xn, w_ref[:, 3 * SB_WIDTH:3 * SB_WIDTH + LRU_WIDTH])
    xg_ref[...] = _dot(xn, w_ref[:, 3 * SB_WIDTH + LRU_WIDTH:])


def _even_in(x, gain, w, ones_bd, q_gain, k_gain):
    n = x.shape[0]
    tm = min(TOKEN_TILE, n)
    width = w.shape[1]
    row = lambda i: (i, 0)
    out = jax.ShapeDtypeStruct((n, SB_WIDTH), F32)
    return pl.pallas_call(
        _even_in_kernel,
        grid=(n // tm,),
        in_specs=[
            pl.BlockSpec((tm, D_MODEL), row),
            _resident((1, D_MODEL)),
            _resident((D_MODEL, width)),
            _resident((SB_WIDTH, SB_WIDTH)),
            _resident((1, SB_WIDTH)),
            _resident((1, SB_WIDTH)),
        ],
        out_specs=[pl.BlockSpec((tm, SB_WIDTH), row)] * 5,
        out_shape=[out] * 5,
        compiler_params=_params("arbitrary"),
        name="even_in",
    )(x, gain.reshape(1, D_MODEL), w, ones_bd, q_gain, k_gain)


def _out_proj_kernel(*refs, n_parts):
    parts, (w_ref, x_ref, o_ref) = refs[:n_parts], refs[n_parts:]
    acc = x_ref[...]
    row = 0
    for p in parts:
        width = p.shape[-1]
        acc = acc + _dot(p[...].astype(BF16), w_ref[row:row + width, :])
        row += width
    o_ref[...] = acc


def _out_proj(parts, w, x):
    n = x.shape[0]
    tm = min(TOKEN_TILE, n)
    row = lambda i: (i, 0)
    return pl.pallas_call(
        functools.partial(_out_proj_kernel, n_parts=len(parts)),
        grid=(n // tm,),
        in_specs=[pl.BlockSpec((tm, p.shape[1]), row) for p in parts]
        + [_resident(w.shape), pl.BlockSpec((tm, D_MODEL), row)],
        out_specs=pl.BlockSpec((tm, D_MODEL), row),
        out_shape=jax.ShapeDtypeStruct((n, D_MODEL), F32),
        compiler_params=_params("arbitrary"),
        name="out_proj",
    )(*parts, w, x)


def _gelu_tanh(x):
    cdf = 0.5 * (1.0 + jnp.tanh(math.sqrt(2.0 / math.pi) * (x + 0.044715 * (x * x * x))))
    return x * cdf


def _lru_coeffs(xc, wa_ref, ba_ref, wi_ref, bi_ref, sp_lam):
    xb = xc.astype(BF16)
    r = jax.nn.sigmoid(_dot(xb, wa_ref[...]) + ba_ref[...])
    i = jax.nn.sigmoid(_dot(xb, wi_ref[...]) + bi_ref[...])
    log_a = -LRU_C * r * sp_lam
    a = jnp.exp(log_a)
    b = jnp.sqrt(-jnp.tanh(log_a) * (a * a + 1.0)) * (i * xc)
    return a, b


def _lru_prompt_kernel(xr_ref, xg_ref, cw_ref, cb_ref, wa_ref, ba_ref, wi_ref, bi_ref, lam_ref,
                       rec_ref, tail_ref, hlast_ref, xp_ref, a_ref, b_ref, h_ref, hc_ref):
    t = pl.program_id(1)
    tt = xr_ref.shape[1]
    pad = V7X_SUBLANES

    @pl.when(t == 0)
    def _():
        xp_ref[0:pad, :] = jnp.zeros((pad, LRU_WIDTH), F32)
        hc_ref[...] = jnp.zeros_like(hc_ref)

    @pl.when(t > 0)
    def _():
        xp_ref[0:pad, :] = xp_ref[tt:tt + pad, :]

    xp_ref[pad:pad + tt, :] = xr_ref[0]
    xc = cw_ref[0:1, :] * xp_ref[pad - 3:pad - 3 + tt, :]
    for j in range(1, CONV_WIDTH):
        xc = xc + cw_ref[j:j + 1, :] * xp_ref[pad - 3 + j:pad - 3 + j + tt, :]
    xc = xc + cb_ref[...]
    a, b = _lru_coeffs(xc, wa_ref, ba_ref, wi_ref, bi_ref, _softplus(-lam_ref[...]))
    a_ref[...] = a
    b_ref[...] = b

    row = lax.broadcasted_iota(jnp.int32, (V7X_SUBLANES, LRU_WIDTH), 0)

    def group(j, h):
        base = pl.multiple_of(j * V7X_SUBLANES, V7X_SUBLANES)
        ag = a_ref[pl.ds(base, V7X_SUBLANES), :]
        bg = b_ref[pl.ds(base, V7X_SUBLANES), :]
        for d in (1, 2, 4):
            keep = row >= d
            a_prev = jnp.where(keep, pltpu.roll(ag, d, axis=0), 1.0)
            b_prev = jnp.where(keep, pltpu.roll(bg, d, axis=0), 0.0)
            bg = ag * b_prev + bg
            ag = ag * a_prev
        hg = ag * h + bg
        h_ref[pl.ds(base, V7X_SUBLANES), :] = hg
        return hg[V7X_SUBLANES - 1:V7X_SUBLANES, :]

    h_end = lax.fori_loop(0, tt // V7X_SUBLANES, group, hc_ref[...])
    hc_ref[...] = h_end
    rec_ref[0] = h_ref[...] * _gelu_tanh(xg_ref[0])

    @pl.when(t == pl.num_programs(1) - 1)
    def _():
        tail_ref[0] = xp_ref[pad + tt - 3:pad + tt, :]
        hlast_ref[0] = h_end


def _lru_prompt(xr, xg, p):
    bsz, t_len, w = xr.shape
    tt = min(LRU_TIME_TILE, t_len)
    tile = pl.BlockSpec((1, tt, w), lambda b, t: (b, t, 0))
    per_b = lambda rows: pl.BlockSpec((1, rows, w), lambda b, t: (b, 0, 0))
    return pl.pallas_call(
        _lru_prompt_kernel,
        grid=(bsz, t_len // tt),
        in_specs=[tile, tile, _resident((CONV_WIDTH, w)), _resident((1, w)), _resident((w, w)),
                  _resident((1, w)), _resident((w, w)), _resident((1, w)), _resident((1, w))],
        out_specs=[tile, per_b(CONV_WIDTH - 1), per_b(1)],
        out_shape=[jax.ShapeDtypeStruct((bsz, t_len, w), F32),
                   jax.ShapeDtypeStruct((bsz, CONV_WIDTH - 1, w), F32),
                   jax.ShapeDtypeStruct((bsz, 1, w), F32)],
        scratch_shapes=[pltpu.VMEM((tt + 2 * V7X_SUBLANES, w), F32), pltpu.VMEM((tt, w), F32),
                        pltpu.VMEM((tt, w), F32), pltpu.VMEM((tt, w), F32), pltpu.VMEM((1, w), F32)],
        compiler_params=_params("arbitrary", "arbitrary"),
        name="lru_prompt",
    )(xr, xg, p["lru_conv_w"], p["lru_conv_b"], p["lru_wa"], p["lru_ba"], p["lru_wi"], p["lru_bi"],
      p["lru_lambda"])


def _lru_sample_kernel(xr_ref, xg_ref, buf_ref, h0_ref, cw_ref, cb_ref, wa_ref, ba_ref, wi_ref, bi_ref,
                       lam_ref, rec_ref, tail_ref, hlast_ref):
    t_len = xr_ref.shape[0]
    xp = [buf_ref[j] for j in range(CONV_WIDTH - 1)] + [xr_ref[j] for j in range(t_len)]
    sp_lam = _softplus(-lam_ref[...])
    h = h0_ref[...]
    for t in range(t_len):
        xc = cw_ref[0:1, :] * xp[t]
        for j in range(1, CONV_WIDTH):
            xc = xc + cw_ref[j:j + 1, :] * xp[t + j]
        xc = xc + cb_ref[...]
        a, b = _lru_coeffs(xc, wa_ref, ba_ref, wi_ref, bi_ref, sp_lam)
        h = a * h + b
        rec_ref[t] = h * _gelu_tanh(xg_ref[t])
    for j in range(CONV_WIDTH - 1):
        tail_ref[j] = xp[t_len + j]
    hlast_ref[...] = h


def _lru_sample(xr, xg, buf, h0, p):
    t_len, bsz, w = xr.shape
    full = lambda shape: pl.BlockSpec(shape, lambda i: (0,) * len(shape))
    return pl.pallas_call(
        _lru_sample_kernel,
        grid=(1,),
        in_specs=[full((t_len, bsz, w)), full((t_len, bsz, w)), full((CONV_WIDTH - 1, bsz, w)), full((bsz, w)),
                  full((CONV_WIDTH, w)), full((1, w)), full((w, w)), full((1, w)), full((w, w)), full((1, w)),
                  full((1, w))],
        out_specs=[full((t_len, bsz, w)), full((CONV_WIDTH - 1, bsz, w)), full((bsz, w))],
        out_shape=[jax.ShapeDtypeStruct((t_len, bsz, w), F32),
                   jax.ShapeDtypeStruct((CONV_WIDTH - 1, bsz, w), F32),
                   jax.ShapeDtypeStruct((bsz, w), F32)],
        compiler_params=_params("arbitrary"),
        name="lru_sample",
    )(xr, xg, buf, h0, p["lru_conv_w"], p["lru_conv_b"], p["lru_wa"], p["lru_ba"], p["lru_wi"], p["lru_bi"],
      p["lru_lambda"])


def _sb_chains(scores, masks, bias, tri, carry):
    zs, cums = [], []
    for s, m in zip(scores, masks):
        z = s + bias
        sp = _softplus(z)
        if m is not None:
            sp = jnp.where(m, sp, 0.0)
        zs.append(z)
        cums.append(_dot(jnp.concatenate(_split2(sp), axis=1), tri))
    ws = []
    for z, c, m in zip(zs, cums, masks):
        w = jnp.exp(z - c - carry)
        if m is not None:
            w = jnp.where(m, w, 0.0)
        ws.append(w.astype(BF16))
        carry = carry + c[:, 0:1]
    return carry, ws


def _sb_prompt_kernel(bias_ref, q_ref, k_ref, v_ref, tri_ref, o_ref):
    hp, qi = pl.program_id(1), pl.program_id(2)
    tq = q_ref.shape[1]
    kw = tq // 2
    q = q_ref[0] * (SB_HEAD_DIM ** -0.5)
    lane = lax.broadcasted_iota(jnp.int32, (tq, 2 * SB_HEAD_DIM), 1)
    q2 = jnp.concatenate([jnp.where(lane < SB_HEAD_DIM, q, 0.0), jnp.where(lane < SB_HEAD_DIM, 0.0, q)], axis=0)
    q2 = q2.astype(BF16)
    first_head = lax.broadcasted_iota(jnp.int32, (2 * tq, 1), 0) < tq
    bias = jnp.where(first_head, bias_ref[hp * 2], bias_ref[hp * 2 + 1])
    tri = tri_ref[...]

    def pair_step(j, masks, carry, acc):
        start = pl.multiple_of(j * tq, tq)
        newer, older = pl.ds(start + kw, kw), pl.ds(start, kw)
        scores = [_dot_nt(q2, k_ref[0, newer, :].astype(BF16)), _dot_nt(q2, k_ref[0, older, :].astype(BF16))]
        carry, ws = _sb_chains(scores, masks, bias, tri, carry)
        acc = acc + _dot(ws[0], v_ref[0, newer, :].astype(BF16)) + _dot(ws[1], v_ref[0, older, :].astype(BF16))
        return carry, acc

    q_pos = lax.broadcasted_iota(jnp.int32, (2 * tq, kw), 0) & (tq - 1)
    k_pos = lax.broadcasted_iota(jnp.int32, (2 * tq, kw), 1)
    state = (jnp.zeros((2 * tq, 1), F32), jnp.zeros((2 * tq, 2 * SB_HEAD_DIM), F32))
    state = pair_step(qi, [k_pos + kw < q_pos, k_pos < q_pos], *state)
    state = lax.fori_loop(0, qi, lambda it, st: pair_step(qi - 1 - it, [None, None], *st), state)
    acc = state[1]
    o_ref[0] = jnp.where(lane < SB_HEAD_DIM, acc[:tq], acc[tq:])


def _sb_prompt(q, k, v, bias, tri):
    bsz, t_len, width = q.shape
    tq = 2 * ATT_TILE
    assert t_len % tq == 0 and tq & (tq - 1) == 0
    pair = 2 * SB_HEAD_DIM
    tile = pl.BlockSpec((1, tq, pair), lambda b, hp, qi: (b, qi, hp))
    seq = pl.BlockSpec((1, t_len, pair), lambda b, hp, qi: (b, 0, hp))
    return pl.pallas_call(
        _sb_prompt_kernel,
        grid=(bsz, width // pair, t_len // tq),
        in_specs=[pl.BlockSpec(memory_space=pltpu.SMEM), tile, seq, seq,
                  pl.BlockSpec((tq, ATT_TILE), lambda b, hp, qi: (0, 0), pipeline_mode=pl.Buffered(1))],
        out_specs=tile,
        out_shape=jax.ShapeDtypeStruct((bsz, t_len, width), F32),
        compiler_params=_params("arbitrary", "arbitrary", "arbitrary"),
        name="sb_prompt",
    )(bias, q, k, v, tri)


def _sb_sample_kernel(pt_ref, q_ref, kn_ref, vn_ref, bias_ref, tri_ref, *refs, n_pages):
    del pt_ref
    k_pages, v_pages, o_ref = refs[:n_pages], refs[n_pages:2 * n_pages], refs[2 * n_pages]
    t_len, width = q_ref.shape[1], q_ref.shape[2]
    rows = t_len * SB_HEADS
    q = q_ref[0] * (SB_HEAD_DIM ** -0.5)
    qb = jnp.broadcast_to(q[:, None, :], (t_len, SB_HEADS, width)).reshape(rows, width)
    row = lax.broadcasted_iota(jnp.int32, (rows, width), 0)
    lane = lax.broadcasted_iota(jnp.int32, (rows, width), 1)
    own_head = (row & (SB_HEADS - 1)) == (lane >> int(math.log2(SB_HEAD_DIM)))
    qbd = jnp.where(own_head, qb, 0.0).astype(BF16)
    page = lambda ref: ref[...].reshape(width, PAGE_SIZE).astype(BF16)

    pad = jnp.zeros((PAGE_SIZE - t_len, width), F32)
    k_new = jnp.concatenate([kn_ref[0], pad], axis=0).astype(BF16)
    v_new = jnp.concatenate([vn_ref[0], pad], axis=0).astype(BF16)
    key = lax.broadcasted_iota(jnp.int32, (rows, PAGE_SIZE), 1)
    tok = lax.broadcasted_iota(jnp.int32, (rows, PAGE_SIZE), 0) >> int(math.log2(SB_HEADS))
    order = list(reversed(range(n_pages)))
    scores = [_dot_nt(qbd, k_new)] + [_dot(qbd, page(k_pages[p])) for p in order]
    masks = [key < tok] + [None] * n_pages
    _, ws = _sb_chains(scores, masks, bias_ref[...], tri_ref[...], jnp.zeros((rows, 1), F32))
    acc = _dot(ws[0], v_new)
    for w, p in zip(ws[1:], order):
        acc = acc + _dot_nt(w, page(v_pages[p]))
    acc = jnp.where(own_head, acc, 0.0)
    o_ref[0] = jnp.sum(acc.reshape(t_len, SB_HEADS, width), axis=1)


def _sb_sample(q, k_new, v_new, cache_k, cache_v, page_table, bias_rows, tri):
    bsz, t_len, width = q.shape
    n_pages = page_table.shape[1]
    rows = t_len * SB_HEADS
    tok = pl.BlockSpec((1, t_len, width), lambda b, pt: (b, 0, 0))
    const = lambda shape: pl.BlockSpec(shape, lambda b, pt: (0, 0), pipeline_mode=pl.Buffered(1))
    pages = [pl.BlockSpec((None, SB_HEADS, SB_HEAD_DIM, PAGE_SIZE), lambda b, pt, j=j: (pt[b, j], 0, 0, 0))
             for j in range(n_pages)]
    grid_spec = pltpu.PrefetchScalarGridSpec(
        num_scalar_prefetch=1,
        grid=(bsz,),
        in_specs=[tok, tok, tok, const((rows, PAGE_SIZE)), const((2 * PAGE_SIZE, PAGE_SIZE))] + pages + pages,
        out_specs=tok,
    )
    return pl.pallas_call(
        functools.partial(_sb_sample_kernel, n_pages=n_pages),
        grid_spec=grid_spec,
        out_shape=jax.ShapeDtypeStruct((bsz, t_len, width), F32),
        compiler_params=_params("arbitrary"),
        name="sb_sample",
    )(page_table, q, k_new, v_new, bias_rows, tri, *([cache_k] * n_pages), *([cache_v] * n_pages))


def _tri_suffix(n):
    idx = jnp.arange(n)
    tri = (idx[:, None] >= idx[None, :]).astype(BF16)
    return jnp.concatenate([tri, tri], axis=0)


ODD_COL_CHUNK = 1024
DN_TIME_TILE = 512
DN_GROUP = 2


def _odd_in_kernel(x_ref, g_ref, w_ref, wba_ref, qkv_ref, z_ref, ba_ref):
    xn = _rms_normalize(x_ref[...], g_ref[...]).astype(BF16)
    for c in range(3 * DN_WIDTH // ODD_COL_CHUNK):
        lo = c * ODD_COL_CHUNK
        qkv_ref[:, lo:lo + ODD_COL_CHUNK] = _dot(xn, w_ref[:, lo:lo + ODD_COL_CHUNK])
    z_ref[...] = _dot(xn, w_ref[:, 3 * DN_WIDTH:])
    ba_ref[...] = _dot(xn, wba_ref[...])


def _odd_in(x, gain, w_qkvz, w_ba):
    n = x.shape[0]
    tm = min(TOKEN_TILE, n)
    row = lambda i: (i, 0)
    return pl.pallas_call(
        _odd_in_kernel,
        grid=(n // tm,),
        in_specs=[pl.BlockSpec((tm, D_MODEL), row), _resident((1, D_MODEL)),
                  _resident(w_qkvz.shape), _resident(w_ba.shape)],
        out_specs=[pl.BlockSpec((tm, 3 * DN_WIDTH), row), pl.BlockSpec((tm, DN_WIDTH), row),
                   pl.BlockSpec((tm, V7X_LANES), row)],
        out_shape=[jax.ShapeDtypeStruct((n, 3 * DN_WIDTH), F32), jax.ShapeDtypeStruct((n, DN_WIDTH), F32),
                   jax.ShapeDtypeStruct((n, V7X_LANES), F32)],
        compiler_params=_params("arbitrary"),
        name="odd_in",
    )(x, gain.reshape(1, D_MODEL), w_qkvz, w_ba)


def _l2_normalize(x):
    return x * lax.rsqrt(jnp.sum(x * x, axis=-1, keepdims=True) + EPS)


def _silu(x):
    return x * jax.nn.sigmoid(x)


def _head_column(x, lane_index):
    lane = lax.broadcasted_iota(jnp.int32, x.shape, 1)
    return jnp.sum(jnp.where(lane == lane_index, x, 0.0), axis=-1, keepdims=True)


def _dn_gates(ba, head, a_log, dt_bias):
    beta = jax.nn.sigmoid(_head_column(ba, head))
    a_logit = _head_column(ba, DN_HEADS + head)
    g = -jnp.exp(jnp.full((1, 1), a_log, F32)) * _softplus(a_logit + dt_bias)
    return beta, g


def _dn_prompt_kernel(alog_ref, dtb_ref, q_ref, k_ref, v_ref, z_ref, ba_ref, cwq_ref, cwk_ref, cwv_ref,
                      gain_ref, low_ref, uo_ref, o_ref, s_ref, xq_ref, xk_ref, xv_ref, state_ref):
    grp, t = pl.program_id(1), pl.program_id(2)
    tt, gw = q_ref.shape[1], q_ref.shape[2]
    dh, cs, pad = DN_HEAD_DIM, DN_CHUNK, V7X_SUBLANES
    n_heads, n_chunks = gw // dh, tt // cs

    @pl.when(t == 0)
    def _():
        state_ref[...] = jnp.zeros_like(state_ref)

    def conv_silu(x_ref, xp_ref, cw_ref):
        @pl.when(t == 0)
        def _():
            xp_ref[0:pad, :] = jnp.zeros((pad, gw), F32)

        @pl.when(t > 0)
        def _():
            xp_ref[0:pad, :] = xp_ref[tt:tt + pad, :]

        xp_ref[pad:pad + tt, :] = x_ref[0]
        y = cw_ref[0:1, :] * xp_ref[pad - 3:pad - 3 + tt, :]
        for j in range(1, CONV_WIDTH):
            y = y + cw_ref[j:j + 1, :] * xp_ref[pad - 3 + j:pad - 3 + j + tt, :]
        return _silu(y)

    q_c, k_c, v_c = conv_silu(q_ref, xq_ref, cwq_ref), conv_silu(k_ref, xk_ref, cwk_ref), conv_silu(v_ref, xv_ref, cwv_ref)
    ba = ba_ref[0]
    per_head = {name: [] for name in ("q", "k", "v", "beta", "g")}
    for h in range(n_heads):
        lanes = slice(h * dh, (h + 1) * dh)
        head = grp * n_heads + h
        beta_h, g_h = _dn_gates(ba, head, alog_ref[head], dtb_ref[head])
        per_head["q"].append(_l2_normalize(q_c[:, lanes]) * (dh ** -0.5))
        per_head["k"].append(_l2_normalize(k_c[:, lanes]))
        per_head["v"].append(v_c[:, lanes])
        per_head["beta"].append(beta_h)
        per_head["g"].append(g_h)
    nb = n_heads * n_chunks
    stack = lambda name: jnp.concatenate(per_head[name], axis=0).reshape(nb, cs, -1)
    qc, kc, vc, beta, g = stack("q"), stack("k"), stack("v"), stack("beta"), stack("g")

    bmm = lambda a, b: lax.dot_general(a, b, (((2,), (1,)), ((0,), (0,))), preferred_element_type=F32)
    bmm_nt = lambda a, b: lax.dot_general(a, b, (((2,), (2,)), ((0,), (0,))), preferred_element_type=F32)
    low = jnp.broadcast_to(low_ref[...][None], (nb, cs, cs))
    row = lax.broadcasted_iota(jnp.int32, (nb, cs, cs), 1)
    col = lax.broadcasted_iota(jnp.int32, (nb, cs, cs), 2)
    parts = _split3(g * uo_ref[...][None])
    gfull = bmm(low, parts[0]) + bmm(low, parts[1]) + bmm(low, parts[2])
    decay = jnp.exp(gfull[:, :, 0:cs])
    g_col = gfull[:, :, 2 * cs:]
    e_g = jnp.exp(g_col)
    g_last = g_col[:, cs - 1:cs, :]
    kb = kc.astype(BF16)
    a_mat = jnp.where(col < row, bmm_nt(kb, kb) * decay, 0.0) * beta
    p_mat = jnp.where(col <= row, bmm_nt(qc.astype(BF16), kb) * decay, 0.0).astype(BF16)
    rhs = jnp.concatenate([vc * beta, kc * (beta * e_g)], axis=2)
    t_off, x = -a_mat, a_mat
    for _ in range(int(math.log2(cs)) - 1):
        xb = x.astype(BF16)
        x = bmm(xb, xb)
        t_off = t_off + x + bmm(t_off.astype(BF16), x.astype(BF16))
    sol = rhs + bmm(t_off.astype(BF16), rhs.astype(BF16))
    uv, wk = sol[:, :, 0:dh], sol[:, :, dh:].astype(BF16)
    qg = (qc * e_g).astype(BF16)
    kg = (kc * jnp.exp(g_last - g_col)).astype(BF16)
    g_end = jnp.exp(g_last)

    states = [state_ref[h] for h in range(n_heads)]
    for c in range(n_chunks):
        rows = slice(c * cs, (c + 1) * cs)
        for h in range(n_heads):
            i = h * n_chunks + c
            sb = states[h].astype(BF16)
            ws = _dot(jnp.concatenate([wk[i], qg[i]], axis=0), sb)
            ub = (uv[i] - ws[0:cs]).astype(BF16)
            o = ws[cs:] + _dot(p_mat[i], ub)
            states[h] = g_end[i] * states[h] + _dot_tn(kg[i], ub)
            lanes = slice(h * dh, (h + 1) * dh)
            o_ref[0, rows, lanes] = _rms_normalize(o, gain_ref[...]) * _silu(z_ref[0, rows, lanes])
    for h in range(n_heads):
        state_ref[h] = states[h]

    @pl.when(t == pl.num_programs(2) - 1)
    def _():
        for h in range(n_heads):
            s_ref[0, h] = states[h]


def _dn_prompt(qkv, z, ba, p):
    bsz, t_len, _ = qkv.shape
    tt = min(DN_TIME_TILE, t_len)
    dh = DN_HEAD_DIM
    gw = DN_GROUP * dh
    groups = DN_HEADS // DN_GROUP
    lane_blk = lambda off: pl.BlockSpec((1, tt, gw), lambda b, g, t: (b, t, off + g))
    cw_blk = lambda off: pl.BlockSpec((CONV_WIDTH, gw), lambda b, g, t: (0, off + g))
    const = lambda shape: pl.BlockSpec(shape, lambda b, g, t: (0,) * len(shape), pipeline_mode=pl.Buffered(1))
    smem = pl.BlockSpec(memory_space=pltpu.SMEM)
    return pl.pallas_call(
        _dn_prompt_kernel,
        grid=(bsz, groups, t_len // tt),
        in_specs=[smem, smem, lane_blk(0), lane_blk(groups), lane_blk(2 * groups), lane_blk(0),
                  pl.BlockSpec((1, tt, V7X_LANES), lambda b, g, t: (b, t, 0)),
                  cw_blk(0), cw_blk(groups), cw_blk(2 * groups),
                  const((1, dh)), const((DN_CHUNK, DN_CHUNK)), const((DN_CHUNK, 4 * DN_CHUNK))],
        out_specs=[lane_blk(0), pl.BlockSpec((1, DN_GROUP, dh, dh), lambda b, g, t: (b, g, 0, 0))],
        out_shape=[jax.ShapeDtypeStruct((bsz, t_len, DN_WIDTH), F32),
                   jax.ShapeDtypeStruct((bsz, DN_HEADS, dh, dh), F32)],
        scratch_shapes=[pltpu.VMEM((tt + 2 * V7X_SUBLANES, gw), F32)] * 3 + [pltpu.VMEM((DN_GROUP, dh, dh), F32)],
        compiler_params=_params("arbitrary", "arbitrary", "arbitrary"),
        name="dn_prompt",
    )(p["dn_A_log"], p["dn_dt_bias"], qkv, qkv, qkv, z, ba, p["dn_conv_w"], p["dn_conv_w"], p["dn_conv_w"],
      p["dn_o_gain"], p["dn_low"], p["dn_uo"])


def _dn_sample_kernel(alog_ref, dtb_ref, qkv_ref, buf_ref, z_ref, ba_ref, s0_ref, cw_ref, gain_ref,
                      o_ref, tail_ref, s_ref):
    t_len = qkv_ref.shape[1]
    dh = DN_HEAD_DIM
    xp = jnp.concatenate([buf_ref[0], qkv_ref[0]], axis=0)
    y = cw_ref[0:1, :] * xp[0:t_len]
    for j in range(1, CONV_WIDTH):
        y = y + cw_ref[j:j + 1, :] * xp[j:j + t_len]
    y = _silu(y)
    tail_ref[0] = xp[t_len:t_len + CONV_WIDTH - 1]
    ba = ba_ref[0]
    for h in range(DN_HEADS):
        q = _l2_normalize(y[:, h * dh:(h + 1) * dh]) * (dh ** -0.5)
        k = _l2_normalize(y[:, DN_WIDTH + h * dh:DN_WIDTH + (h + 1) * dh])
        v = y[:, 2 * DN_WIDTH + h * dh:2 * DN_WIDTH + (h + 1) * dh]
        beta, g = _dn_gates(ba, h, alog_ref[h], dtb_ref[h])
        cols = jnp.concatenate([q, k], axis=0).T
        state = s0_ref[0, h]
        outs = []
        for t in range(t_len):
            q_col, k_col = cols[:, t:t + 1], cols[:, t_len + t:t_len + t + 1]
            state = jnp.exp(g[t:t + 1, :]) * state
            u = beta[t:t + 1, :] * (v[t:t + 1, :] - jnp.sum(k_col * state, axis=0, keepdims=True))
            state = state + k_col * u
            outs.append(jnp.sum(q_col * state, axis=0, keepdims=True))
        s_ref[0, h] = state
        o = jnp.concatenate(outs, axis=0)
        o_ref[0, :, h * dh:(h + 1) * dh] = _rms_normalize(o, gain_ref[...]) * _silu(z_ref[0, :, h * dh:(h + 1) * dh])


def _dn_sample(qkv, buf, z, ba, s0, p):
    bsz, t_len, _ = qkv.shape
    dh = DN_HEAD_DIM
    per_b = lambda *shape: pl.BlockSpec((1,) + shape, lambda b: (b,) + (0,) * len(shape))
    const = lambda shape: pl.BlockSpec(shape, lambda b: (0,) * len(shape), pipeline_mode=pl.Buffered(1))
    smem = pl.BlockSpec(memory_space=pltpu.SMEM)
    return pl.pallas_call(
        _dn_sample_kernel,
        grid=(bsz,),
        in_specs=[smem, smem, per_b(t_len, 3 * DN_WIDTH), per_b(CONV_WIDTH - 1, 3 * DN_WIDTH), per_b(t_len, DN_WIDTH),
                  per_b(t_len, V7X_LANES), per_b(DN_HEADS, dh, dh), const((CONV_WIDTH, 3 * DN_WIDTH)), const((1, dh))],
        out_specs=[per_b(t_len, DN_WIDTH), per_b(CONV_WIDTH - 1, 3 * DN_WIDTH), per_b(DN_HEADS, dh, dh)],
        out_shape=[jax.ShapeDtypeStruct((bsz, t_len, DN_WIDTH), F32),
                   jax.ShapeDtypeStruct((bsz, CONV_WIDTH - 1, 3 * DN_WIDTH), F32),
                   jax.ShapeDtypeStruct((bsz, DN_HEADS, dh, dh), F32)],
        compiler_params=_params("arbitrary"),
        name="dn_sample",
    )(p["dn_A_log"], p["dn_dt_bias"], qkv, buf, z, ba, s0, p["dn_conv_w"], p["dn_o_gain"])


def _block_diag(w):
    n, c, d = w.shape
    eye = jnp.eye(n, dtype=w.dtype)
    return (eye[:, None, :, None] * w[:, :, None, :]).reshape(n * c, n * d)


def _prepare_weights(d):
    p = {}
    for name in ("w_ffn1_in", "w_ffn1_out", "w_ffn2_in", "w_ffn2_out"):
        p[name] = d[name].astype(BF16)
    p["w_in_even"] = d["w_in_even"][0].astype(BF16)
    p["w_out_even"] = d["w_out_even"][0].astype(BF16)
    p["ones_bd"] = _block_diag(jnp.ones((SB_HEADS, SB_HEAD_DIM, SB_HEAD_DIM), BF16))
    p["q_gain"] = jnp.tile(d["sb_q_gain"][0], SB_HEADS).reshape(1, SB_WIDTH)
    p["k_gain"] = jnp.tile(d["sb_k_gain"][0], SB_HEADS).reshape(1, SB_WIDTH)
    p["lru_conv_w"] = d["lru_conv_w"][0]
    p["lru_conv_b"] = d["lru_conv_b"][0].reshape(1, LRU_WIDTH)
    p["lru_wa"] = _block_diag(d["lru_w_a"][0]).astype(BF16)
    p["lru_wi"] = _block_diag(d["lru_w_i"][0]).astype(BF16)
    p["lru_ba"] = d["lru_b_a"][0].reshape(1, LRU_WIDTH)
    p["lru_bi"] = d["lru_b_i"][0].reshape(1, LRU_WIDTH)
    p["lru_lambda"] = d["lru_lambda"][0].reshape(1, LRU_WIDTH)
    w_odd = d["w_in_odd"][0]
    p["w_qkvz"] = w_odd[:, :4 * DN_WIDTH].astype(BF16)
    p["w_ba"] = jnp.pad(w_odd[:, 4 * DN_WIDTH:], ((0, 0), (0, V7X_LANES - 2 * DN_HEADS))).astype(BF16)
    p["w_out_odd"] = d["w_out_odd"][0].astype(BF16)
    p["dn_conv_w"] = d["dn_conv_w"][0]
    p["dn_A_log"] = d["dn_A_log"][0]
    p["dn_dt_bias"] = d["dn_dt_bias"][0]
    p["dn_o_gain"] = d["dn_o_gain"][0].reshape(1, DN_HEAD_DIM)
    idx = jnp.arange(DN_CHUNK)
    p["dn_low"] = (idx[None, :] <= idx[:, None]).astype(BF16)
    later = (idx[:, None] > idx[None, :]).astype(F32)
    p["dn_uo"] = jnp.concatenate([later, jnp.zeros((DN_CHUNK, DN_CHUNK), F32), jnp.ones((DN_CHUNK, 2 * DN_CHUNK), F32)], axis=1)
    return p


def _even_layer_mixer(x, group, d, p):
    kind, bsz, t_len = group
    q, k, v, xr, xg = _even_in(x, d["norm_mix"][0], p["w_in_even"], p["ones_bd"], p["q_gain"], p["k_gain"])
    seq = lambda a: a.reshape(bsz, t_len, a.shape[-1])
    if kind == "prompt":
        attn = _sb_prompt(seq(q), seq(k), seq(v), d["sb_bias"][0], _tri_suffix(ATT_TILE))
        rec, tail, h_last = _lru_prompt(seq(xr), seq(xg), p)
        h_last = h_last[:, 0]
    else:
        bias_rows = jnp.broadcast_to(jnp.tile(d["sb_bias"][0], t_len)[:, None], (t_len * SB_HEADS, PAGE_SIZE))
        cache_k = jnp.transpose(d["cache_k"][0], (0, 2, 3, 1))
        cache_v = jnp.transpose(d["cache_v"][0], (0, 2, 3, 1))
        attn = _sb_sample(seq(q), seq(k), seq(v), cache_k, cache_v, d["page_table"], bias_rows, _tri_suffix(PAGE_SIZE))
        tm = lambda a: jnp.swapaxes(a, 0, 1)
        rec, tail, h_last = _lru_sample(tm(seq(xr)), tm(seq(xg)), tm(d["state_lru_conv"][0]), d["state_lru_h"][0], p)
        rec, tail = tm(rec), tm(tail)
    x = _out_proj([attn.reshape(-1, SB_WIDTH), rec.reshape(-1, LRU_WIDTH)], p["w_out_even"], x)
    heads = lambda a: a.reshape(bsz, t_len, SB_HEADS, SB_HEAD_DIM)
    return x, (heads(k), heads(v), tail, h_last)


def _odd_layer_mixer(x, group, d, p):
    kind, bsz, t_len = group
    qkv, z, ba = _odd_in(x, d["norm_mix"][1], p["w_qkvz"], p["w_ba"])
    seq = lambda a: a.reshape(bsz, t_len, a.shape[-1])
    if kind == "prompt":
        o, state = _dn_prompt(seq(qkv), seq(z), seq(ba), p)
        tail = seq(qkv)[:, t_len - (CONV_WIDTH - 1):, :]
    else:
        o, tail, state = _dn_sample(seq(qkv), d["state_dn_conv"][0], seq(z), seq(ba), d["state_dn_S"][0], p)
    x = _out_proj([o.reshape(-1, DN_WIDTH)], p["w_out_odd"], x)
    return x, (tail, state)


def kernel(x_prompt, x_sample, cache_k, cache_v, state_lru_conv, state_lru_h, state_dn_conv, state_dn_S, page_table, norm_ffn1, w_ffn1_in, w_ffn1_out, norm_mix, norm_ffn2, w_ffn2_in, w_ffn2_out, w_in_even, sb_q_gain, sb_k_gain, sb_bias, lru_conv_w, lru_conv_b, lru_w_a, lru_b_a, lru_w_i, lru_b_i, lru_lambda, w_out_even, w_in_odd, dn_conv_w, dn_A_log, dn_dt_bias, dn_o_gain, w_out_odd):
    d = dict(locals())
    assert norm_ffn1.shape[0] == 2 and w_in_even.shape[0] == 1 and w_in_odd.shape[0] == 1, "two-layer trunk only"
    p = _prepare_weights(d)
    results = []
    for kind, x in (("prompt", x_prompt), ("sample", x_sample)):
        bsz, t_len, _ = x.shape
        group = (kind, bsz, t_len)
        x = x.reshape(bsz * t_len, D_MODEL)
        x = _ffn(x, norm_ffn1[0], p["w_ffn1_in"][0], p["w_ffn1_out"][0])
        x, even_state = _even_layer_mixer(x, group, d, p)
        x = _ffn(x, norm_ffn2[0], p["w_ffn2_in"][0], p["w_ffn2_out"][0])
        x = _ffn(x, norm_ffn1[1], p["w_ffn1_in"][1], p["w_ffn1_out"][1])
        x, odd_state = _odd_layer_mixer(x, group, d, p)
        x = _ffn(x, norm_ffn2[1], p["w_ffn2_in"][1], p["w_ffn2_out"][1])
        results.append((x.reshape(bsz, t_len, D_MODEL),) + tuple(s[None] for s in even_state + odd_state))
    (yp, *prompt_state), (ys, *sample_state) = results
    return (yp, ys, *prompt_state, *sample_state)
```

```python
import functools
import math

import jax
import jax.numpy as jnp
from jax import lax
from jax.experimental import pallas as pl
from jax.experimental.pallas import tpu as pltpu

F32 = jnp.float32
BF16 = jnp.bfloat16

V7X_LANES = 128
V7X_SUBLANES = 8
V7X_VMEM_LIMIT_BYTES = 56 * 1024 * 1024

D_MODEL = 1024
D_FF = 2 * D_MODEL
SB_HEADS = 8
SB_HEAD_DIM = 64
SB_WIDTH = SB_HEADS * SB_HEAD_DIM
LRU_WIDTH = 512
LRU_BLOCKS = 8
LRU_C = 8.0
CONV_WIDTH = 4
DN_HEADS = 8
DN_HEAD_DIM = 128
DN_WIDTH = DN_HEADS * DN_HEAD_DIM
DN_CHUNK = 64
PAGE_SIZE = 128
EPS = 1e-6

TOKEN_TILE = 512
FF_CHUNK = 512
ATT_TILE = 256
LRU_TIME_TILE = 512


def _params(*sem):
    return pltpu.CompilerParams(dimension_semantics=sem, vmem_limit_bytes=V7X_VMEM_LIMIT_BYTES)


def _resident(shape):
    nd = len(shape)
    return pl.BlockSpec(shape, lambda *_: (0,) * nd, pipeline_mode=pl.Buffered(1))


def _rms_normalize(x, gain):
    ms = jnp.mean(x * x, axis=-1, keepdims=True)
    return x * lax.rsqrt(ms + EPS) * gain


LOG2E = 1.0 / math.log(2.0)


def _softplus(x):
    return jnp.maximum(x, 0.0) + jnp.log(1.0 + jnp.exp2(jnp.abs(x) * (-LOG2E)))


def _split2(x):
    hi = x.astype(BF16)
    lo = (x - hi.astype(F32)).astype(BF16)
    return hi, lo


def _split3(x):
    hi = x.astype(BF16)
    r1 = x - hi.astype(F32)
    mid = r1.astype(BF16)
    lo = (r1 - mid.astype(F32)).astype(BF16)
    return hi, mid, lo


def _dot(a, b):
    return jnp.dot(a, b, preferred_element_type=F32)


def _dot_nt(a, b):
    return lax.dot_general(a, b, (((1,), (1,)), ((), ())), preferred_element_type=F32)


def _dot_tn(a, b):
    return lax.dot_general(a, b, (((0,), (0,)), ((), ())), preferred_element_type=F32)


def _ffn_kernel(x_ref, g_ref, win_ref, wout_ref, o_ref, act_ref):
    x = x_ref[...]
    xn = _rms_normalize(x, g_ref[...]).astype(BF16)
    for c in range(D_FF // FF_CHUNK):
        lo = c * FF_CHUNK
        gate = _dot(xn, win_ref[:, lo:lo + FF_CHUNK])
        up = _dot(xn, win_ref[:, D_FF + lo:D_FF + lo + FF_CHUNK])
        act_ref[:, lo:lo + FF_CHUNK] = (gate * jax.nn.sigmoid(gate) * up).astype(BF16)
    o_ref[...] = x + 0.5 * _dot(act_ref[...], wout_ref[...])


def _ffn(x, gain, w_in, w_out):
    n = x.shape[0]
    tm = min(TOKEN_TILE, n)
    return pl.pallas_call(
        _ffn_kernel,
        grid=(n // tm,),
        in_specs=[
            pl.BlockSpec((tm, D_MODEL), lambda i: (i, 0)),
            _resident((1, D_MODEL)),
            _resident((D_MODEL, 2 * D_FF)),
            _resident((D_FF, D_MODEL)),
        ],
        out_specs=pl.BlockSpec((tm, D_MODEL), lambda i: (i, 0)),
        out_shape=jax.ShapeDtypeStruct((n, D_MODEL), F32),
        scratch_shapes=[pltpu.VMEM((tm, D_FF), BF16)],
        compiler_params=_params("arbitrary"),
        name="ffn",
    )(x, gain.reshape(1, D_MODEL), w_in, w_out)


def _head_rms(x, ones_bd, gain):
    hi, lo = _split2(x * x)
    ms = (_dot(hi, ones_bd) + _dot(lo, ones_bd)) * (1.0 / SB_HEAD_DIM)
    return x * lax.rsqrt(ms + EPS) * gain


def _even_in_kernel(x_ref, g_ref, w_ref, ones_ref, qg_ref, kg_ref,
                    q_ref, k_ref, v_ref, xr_ref, xg_ref):
    xn = _rms_normalize(x_ref[...], g_ref[...]).astype(BF16)
    ones_bd = ones_ref[...]
    q = _dot(xn, w_ref[:, 0:SB_WIDTH])
    q_ref[...] = _head_rms(q, ones_bd, qg_ref[...])
    k = _dot(xn, w_ref[:, SB_WIDTH:2 * SB_WIDTH])
    k_ref[...] = _head_rms(k, ones_bd, kg_ref[...])
    v_ref[...] = _dot(xn, w_ref[:, 2 * SB_WIDTH:3 * SB_WIDTH])
    xr_ref[...] = _dot(xn, w_ref[:, 3 * SB_WIDTH:3 * SB_WIDTH + LRU_WIDTH])
    xg_ref[...] = _dot(xn, w_ref[:, 3 * SB_WIDTH + LRU_WIDTH:])


def _even_in(x, gain, w, ones_bd, q_gain, k_gain):
    n = x.shape[0]
    tm = min(TOKEN_TILE, n)
    width = w.shape[1]
    row = lambda i: (i, 0)
    out = jax.ShapeDtypeStruct((n, SB_WIDTH), F32)
    return pl.pallas_call(
        _even_in_kernel,
        grid=(n // tm,),
        in_specs=[
            pl.BlockSpec((tm, D_MODEL), row),
            _resident((1, D_MODEL)),
            _resident((D_MODEL, width)),
            _resident((SB_WIDTH, SB_WIDTH)),
            _resident((1, SB_WIDTH)),
            _resident((1, SB_WIDTH)),
        ],
        out_specs=[pl.BlockSpec((tm, SB_WIDTH), row)] * 5,
        out_shape=[out] * 5,
        compiler_params=_params("arbitrary"),
        name="even_in",
    )(x, gain.reshape(1, D_MODEL), w, ones_bd, q_gain, k_gain)


def _out_proj_kernel(*refs, n_parts):
    parts, (w_ref, x_ref, o_ref) = refs[:n_parts], refs[n_parts:]
    acc = x_ref[...]
    row = 0
    for p in parts:
        width = p.shape[-1]
        acc = acc + _dot(p[...].astype(BF16), w_ref[row:row + width, :])
        row += width
    o_ref[...] = acc


def _out_proj(parts, w, x):
    n = x.shape[0]
    tm = min(TOKEN_TILE, n)
    row = lambda i: (i, 0)
    return pl.pallas_call(
        functools.partial(_out_proj_kernel, n_parts=len(parts)),
        grid=(n // tm,),
        in_specs=[pl.BlockSpec((tm, p.shape[1]), row) for p in parts]
        + [_resident(w.shape), pl.BlockSpec((tm, D_MODEL), row)],
        out_specs=pl.BlockSpec((tm, D_MODEL), row),
        out_shape=jax.ShapeDtypeStruct((n, D_MODEL), F32),
        compiler_params=_params("arbitrary"),
        name="out_proj",
    )(*parts, w, x)


def _gelu_tanh(x):
    cdf = 0.5 * (1.0 + jnp.tanh(math.sqrt(2.0 / math.pi) * (x + 0.044715 * (x * x * x))))
    return x * cdf


def _lru_coeffs(xc, wa_ref, ba_ref, wi_ref, bi_ref, sp_lam):
    xb = xc.astype(BF16)
    r = jax.nn.sigmoid(_dot(xb, wa_ref[...]) + ba_ref[...])
    i = jax.nn.sigmoid(_dot(xb, wi_ref[...]) + bi_ref[...])
    log_a = -LRU_C * r * sp_lam
    a = jnp.exp(log_a)
    b = jnp.sqrt(-jnp.tanh(log_a) * (a * a + 1.0)) * (i * xc)
    return a, b


def _lru_prompt_kernel(xr_ref, xg_ref, cw_ref, cb_ref, wa_ref, ba_ref, wi_ref, bi_ref, lam_ref,
                       rec_ref, tail_ref, hlast_ref, xp_ref, a_ref, b_ref, h_ref, hc_ref):
    t = pl.program_id(1)
    tt = xr_ref.shape[1]
    pad = V7X_SUBLANES

    @pl.when(t == 0)
    def _():
        xp_ref[0:pad, :] = jnp.zeros((pad, LRU_WIDTH), F32)
        hc_ref[...] = jnp.zeros_like(hc_ref)

    @pl.when(t > 0)
    def _():
        xp_ref[0:pad, :] = xp_ref[tt:tt + pad, :]

    xp_ref[pad:pad + tt, :] = xr_ref[0]
    xc = cw_ref[0:1, :] * xp_ref[pad - 3:pad - 3 + tt, :]
    for j in range(1, CONV_WIDTH):
        xc = xc + cw_ref[j:j + 1, :] * xp_ref[pad - 3 + j:pad - 3 + j + tt, :]
    xc = xc + cb_ref[...]
    a, b = _lru_coeffs(xc, wa_ref, ba_ref, wi_ref, bi_ref, _softplus(-lam_ref[...]))
    a_ref[...] = a
    b_ref[...] = b

    row = lax.broadcasted_iota(jnp.int32, (V7X_SUBLANES, LRU_WIDTH), 0)

    def group(j, h):
        base = pl.multiple_of(j * V7X_SUBLANES, V7X_SUBLANES)
        ag = a_ref[pl.ds(base, V7X_SUBLANES), :]
        bg = b_ref[pl.ds(base, V7X_SUBLANES), :]
        for d in (1, 2, 4):
            keep = row >= d
            a_prev = jnp.where(keep, pltpu.roll(ag, d, axis=0), 1.0)
            b_prev = jnp.where(keep, pltpu.roll(bg, d, axis=0), 0.0)
            bg = ag * b_prev + bg
            ag = ag * a_prev
        hg = ag * h + bg
        h_ref[pl.ds(base, V7X_SUBLANES), :] = hg
        return hg[V7X_SUBLANES - 1:V7X_SUBLANES, :]

    h_end = lax.fori_loop(0, tt // V7X_SUBLANES, group, hc_ref[...])
    hc_ref[...] = h_end
    rec_ref[0] = h_ref[...] * _gelu_tanh(xg_ref[0])

    @pl.when(t == pl.num_programs(1) - 1)
    def _():
        tail_ref[0] = xp_ref[pad + tt - 3:pad + tt, :]
        hlast_ref[0] = h_end


def _lru_prompt(xr, xg, p):
    bsz, t_len, w = xr.shape
    tt = min(LRU_TIME_TILE, t_len)
    tile = pl.BlockSpec((1, tt, w), lambda b, t: (b, t, 0))
    per_b = lambda rows: pl.BlockSpec((1, rows, w), lambda b, t: (b, 0, 0))
    return pl.pallas_call(
        _lru_prompt_kernel,
        grid=(bsz, t_len // tt),
        in_specs=[tile, tile, _resident((CONV_WIDTH, w)), _resident((1, w)), _resident((w, w)),
                  _resident((1, w)), _resident((w, w)), _resident((1, w)), _resident((1, w))],
        out_specs=[tile, per_b(CONV_WIDTH - 1), per_b(1)],
        out_shape=[jax.ShapeDtypeStruct((bsz, t_len, w), F32),
                   jax.ShapeDtypeStruct((bsz, CONV_WIDTH - 1, w), F32),
                   jax.ShapeDtypeStruct((bsz, 1, w), F32)],
        scratch_shapes=[pltpu.VMEM((tt + 2 * V7X_SUBLANES, w), F32), pltpu.VMEM((tt, w), F32),
                        pltpu.VMEM((tt, w), F32), pltpu.VMEM((tt, w), F32), pltpu.VMEM((1, w), F32)],
        compiler_params=_params("arbitrary", "arbitrary"),
        name="lru_prompt",
    )(xr, xg, p["lru_conv_w"], p["lru_conv_b"], p["lru_wa"], p["lru_ba"], p["lru_wi"], p["lru_bi"],
      p["lru_lambda"])


def _lru_sample_kernel(xr_ref, xg_ref, buf_ref, h0_ref, cw_ref, cb_ref, wa_ref, ba_ref, wi_ref, bi_ref,
                       lam_ref, rec_ref, tail_ref, hlast_ref):
    t_len = xr_ref.shape[0]
    xp = [buf_ref[j] for j in range(CONV_WIDTH - 1)] + [xr_ref[j] for j in range(t_len)]
    sp_lam = _softplus(-lam_ref[...])
    h = h0_ref[...]
    for t in range(t_len):
        xc = cw_ref[0:1, :] * xp[t]
        for j in range(1, CONV_WIDTH):
            xc = xc + cw_ref[j:j + 1, :] * xp[t + j]
        xc = xc + cb_ref[...]
        a, b = _lru_coeffs(xc, wa_ref, ba_ref, wi_ref, bi_ref, sp_lam)
        h = a * h + b
        rec_ref[t] = h * _gelu_tanh(xg_ref[t])
    for j in range(CONV_WIDTH - 1):
        tail_ref[j] = xp[t_len + j]
    hlast_ref[...] = h


def _lru_sample(xr, xg, buf, h0, p):
    t_len, bsz, w = xr.shape
    full = lambda shape: pl.BlockSpec(shape, lambda i: (0,) * len(shape))
    return pl.pallas_call(
        _lru_sample_kernel,
        grid=(1,),
        in_specs=[full((t_len, bsz, w)), full((t_len, bsz, w)), full((CONV_WIDTH - 1, bsz, w)), full((bsz, w)),
                  full((CONV_WIDTH, w)), full((1, w)), full((w, w)), full((1, w)), full((w, w)), full((1, w)),
                  full((1, w))],
        out_specs=[full((t_len, bsz, w)), full((CONV_WIDTH - 1, bsz, w)), full((bsz, w))],
        out_shape=[jax.ShapeDtypeStruct((t_len, bsz, w), F32),
                   jax.ShapeDtypeStruct((CONV_WIDTH - 1, bsz, w), F32),
                   jax.ShapeDtypeStruct((bsz, w), F32)],
        compiler_params=_params("arbitrary"),
        name="lru_sample",
    )(xr, xg, buf, h0, p["lru_conv_w"], p["lru_conv_b"], p["lru_wa"], p["lru_ba"], p["lru_wi"], p["lru_bi"],
      p["lru_lambda"])


def _sb_chains(scores, masks, bias, tri, carry, one_dot):
    rows = scores[0].shape[0]
    zs, parts = [], []
    for s, m in zip(scores, masks):
        z = s + bias
        sp = _softplus(z)
        if m is not None:
            sp = jnp.where(m, sp, 0.0)
        zs.append(z)
        parts.append(jnp.concatenate(_split2(sp), axis=1))
    if one_dot:
        cum = _dot(jnp.concatenate(parts, axis=0), tri)
        cums = [cum[i * rows:(i + 1) * rows] for i in range(len(parts))]
    else:
        cums = [_dot(part, tri) for part in parts]
    ws = []
    for z, c, m in zip(zs, cums, masks):
        w = jnp.exp(z - c - carry)
        if m is not None:
            w = jnp.where(m, w, 0.0)
        ws.append(w.astype(BF16))
        carry = carry + c[:, 0:1]
    return carry, ws


def _sb_prompt_kernel(bias_ref, q_ref, k_ref, v_ref, tri_ref, o_ref):
    hp, qi = pl.program_id(1), pl.program_id(2)
    tq = q_ref.shape[1]
    kw = tq // 2
    q = q_ref[0] * (SB_HEAD_DIM ** -0.5)
    lane = lax.broadcasted_iota(jnp.int32, (tq, 2 * SB_HEAD_DIM), 1)
    q2 = jnp.concatenate([jnp.where(lane < SB_HEAD_DIM, q, 0.0), jnp.where(lane < SB_HEAD_DIM, 0.0, q)], axis=0)
    q2 = q2.astype(BF16)
    first_head = lax.broadcasted_iota(jnp.int32, (2 * tq, 1), 0) < tq
    bias = jnp.where(first_head, bias_ref[hp * 2], bias_ref[hp * 2 + 1])
    tri = tri_ref[...]

    def pair_step(j, masks, carry, acc):
        start = pl.multiple_of(j * tq, tq)
        newer, older = pl.ds(start + kw, kw), pl.ds(start, kw)
        scores = [_dot_nt(q2, k_ref[0, newer, :].astype(BF16)), _dot_nt(q2, k_ref[0, older, :].astype(BF16))]
        carry, ws = _sb_chains(scores, masks, bias, tri, carry, one_dot=False)
        acc = acc + _dot(ws[0], v_ref[0, newer, :].astype(BF16)) + _dot(ws[1], v_ref[0, older, :].astype(BF16))
        return carry, acc

    q_pos = lax.broadcasted_iota(jnp.int32, (2 * tq, kw), 0) & (tq - 1)
    k_pos = lax.broadcasted_iota(jnp.int32, (2 * tq, kw), 1)
    state = (jnp.zeros((2 * tq, 1), F32), jnp.zeros((2 * tq, 2 * SB_HEAD_DIM), F32))
    state = pair_step(qi, [k_pos + kw < q_pos, k_pos < q_pos], *state)
    state = lax.fori_loop(0, qi, lambda it, st: pair_step(qi - 1 - it, [None, None], *st), state)
    acc = state[1]
    o_ref[0] = jnp.where(lane < SB_HEAD_DIM, acc[:tq], acc[tq:])


def _sb_prompt(q, k, v, bias, tri):
    bsz, t_len, width = q.shape
    tq = 2 * ATT_TILE
    assert t_len % tq == 0 and tq & (tq - 1) == 0
    pair = 2 * SB_HEAD_DIM
    tile = pl.BlockSpec((1, tq, pair), lambda b, hp, qi: (b, qi, hp))
    seq = pl.BlockSpec((1, t_len, pair), lambda b, hp, qi: (b, 0, hp))
    return pl.pallas_call(
        _sb_prompt_kernel,
        grid=(bsz, width // pair, t_len // tq),
        in_specs=[pl.BlockSpec(memory_space=pltpu.SMEM), tile, seq, seq,
                  pl.BlockSpec((tq, ATT_TILE), lambda b, hp, qi: (0, 0), pipeline_mode=pl.Buffered(1))],
        out_specs=tile,
        out_shape=jax.ShapeDtypeStruct((bsz, t_len, width), F32),
        compiler_params=_params("arbitrary", "arbitrary", "arbitrary"),
        name="sb_prompt",
    )(bias, q, k, v, tri)


def _sb_sample_kernel(pt_ref, q_ref, kn_ref, vn_ref, bias_ref, tri_ref, *refs, n_pages):
    del pt_ref
    k_pages, v_pages, o_ref = refs[:n_pages], refs[n_pages:2 * n_pages], refs[2 * n_pages]
    t_len, width = q_ref.shape[1], q_ref.shape[2]
    rows = t_len * SB_HEADS
    q = q_ref[0] * (SB_HEAD_DIM ** -0.5)
    qb = jnp.broadcast_to(q[:, None, :], (t_len, SB_HEADS, width)).reshape(rows, width)
    row = lax.broadcasted_iota(jnp.int32, (rows, width), 0)
    lane = lax.broadcasted_iota(jnp.int32, (rows, width), 1)
    own_head = (row & (SB_HEADS - 1)) == (lane >> int(math.log2(SB_HEAD_DIM)))
    qbd = jnp.where(own_head, qb, 0.0).astype(BF16)
    page = lambda ref: ref[...].reshape(width, PAGE_SIZE).astype(BF16)

    pad = jnp.zeros((PAGE_SIZE - t_len, width), F32)
    k_new = jnp.concatenate([kn_ref[0], pad], axis=0).astype(BF16)
    v_new = jnp.concatenate([vn_ref[0], pad], axis=0).astype(BF16)
    key = lax.broadcasted_iota(jnp.int32, (rows, PAGE_SIZE), 1)
    tok = lax.broadcasted_iota(jnp.int32, (rows, PAGE_SIZE), 0) >> int(math.log2(SB_HEADS))
    order = list(reversed(range(n_pages)))
    scores = [_dot_nt(qbd, k_new)] + [_dot(qbd, page(k_pages[p])) for p in order]
    masks = [key < tok] + [None] * n_pages
    _, ws = _sb_chains(scores, masks, bias_ref[...], tri_ref[...], jnp.zeros((rows, 1), F32), one_dot=True)
    acc = _dot(ws[0], v_new)
    for w, p in zip(ws[1:], order):
        acc = acc + _dot_nt(w, page(v_pages[p]))
    acc = jnp.where(own_head, acc, 0.0)
    o_ref[0] = jnp.sum(acc.reshape(t_len, SB_HEADS, width), axis=1)


def _sb_sample(q, k_new, v_new, cache_k, cache_v, page_table, bias_rows, tri):
    bsz, t_len, width = q.shape
    n_pages = page_table.shape[1]
    rows = t_len * SB_HEADS
    tok = pl.BlockSpec((1, t_len, width), lambda b, pt: (b, 0, 0))
    const = lambda shape: pl.BlockSpec(shape, lambda b, pt: (0, 0), pipeline_mode=pl.Buffered(1))
    pages = [pl.BlockSpec((None, SB_HEADS, SB_HEAD_DIM, PAGE_SIZE), lambda b, pt, j=j: (pt[b, j], 0, 0, 0))
             for j in range(n_pages)]
    grid_spec = pltpu.PrefetchScalarGridSpec(
        num_scalar_prefetch=1,
        grid=(bsz,),
        in_specs=[tok, tok, tok, const((rows, PAGE_SIZE)), const((2 * PAGE_SIZE, PAGE_SIZE))] + pages + pages,
        out_specs=tok,
    )
    return pl.pallas_call(
        functools.partial(_sb_sample_kernel, n_pages=n_pages),
        grid_spec=grid_spec,
        out_shape=jax.ShapeDtypeStruct((bsz, t_len, width), F32),
        compiler_params=_params("arbitrary"),
        name="sb_sample",
    )(page_table, q, k_new, v_new, bias_rows, tri, *([cache_k] * n_pages), *([cache_v] * n_pages))


def _tri_suffix(n):
    idx = jnp.arange(n)
    tri = (idx[:, None] >= idx[None, :]).astype(BF16)
    return jnp.concatenate([tri, tri], axis=0)


ODD_COL_CHUNK = 1024
DN_TIME_TILE = 512
DN_GROUP = 2


def _odd_in_kernel(x_ref, g_ref, w_ref, wba_ref, qkv_ref, z_ref, ba_ref):
    xn = _rms_normalize(x_ref[...], g_ref[...]).astype(BF16)
    for c in range(3 * DN_WIDTH // ODD_COL_CHUNK):
        lo = c * ODD_COL_CHUNK
        qkv_ref[:, lo:lo + ODD_COL_CHUNK] = _dot(xn, w_ref[:, lo:lo + ODD_COL_CHUNK])
    z_ref[...] = _dot(xn, w_ref[:, 3 * DN_WIDTH:])
    ba_ref[...] = _dot(xn, wba_ref[...])


def _odd_in(x, gain, w_qkvz, w_ba):
    n = x.shape[0]
    tm = min(TOKEN_TILE, n)
    row = lambda i: (i, 0)
    return pl.pallas_call(
        _odd_in_kernel,
        grid=(n // tm,),
        in_specs=[pl.BlockSpec((tm, D_MODEL), row), _resident((1, D_MODEL)),
                  _resident(w_qkvz.shape), _resident(w_ba.shape)],
        out_specs=[pl.BlockSpec((tm, 3 * DN_WIDTH), row), pl.BlockSpec((tm, DN_WIDTH), row),
                   pl.BlockSpec((tm, V7X_LANES), row)],
        out_shape=[jax.ShapeDtypeStruct((n, 3 * DN_WIDTH), F32), jax.ShapeDtypeStruct((n, DN_WIDTH), F32),
                   jax.ShapeDtypeStruct((n, V7X_LANES), F32)],
        compiler_params=_params("arbitrary"),
        name="odd_in",
    )(x, gain.reshape(1, D_MODEL), w_qkvz, w_ba)


def _l2_normalize(x):
    return x * lax.rsqrt(jnp.sum(x * x, axis=-1, keepdims=True) + EPS)


def _silu(x):
    return x * jax.nn.sigmoid(x)


def _head_column(x, lane_index):
    lane = lax.broadcasted_iota(jnp.int32, x.shape, 1)
    return jnp.sum(jnp.where(lane == lane_index, x, 0.0), axis=-1, keepdims=True)


def _dn_gates(ba, alog_row, dtb_row):
    beta = jax.nn.sigmoid(ba)
    g = -jnp.exp(alog_row) * _softplus(ba + dtb_row)
    return beta, g


def _dn_prompt_kernel(alog_ref, dtb_ref, q_ref, k_ref, v_ref, z_ref, ba_ref, cwq_ref, cwk_ref, cwv_ref,
                      gain_ref, low_ref, uo_ref, o_ref, s_ref, xq_ref, xk_ref, xv_ref, state_ref):
    grp, t = pl.program_id(1), pl.program_id(2)
    tt, gw = q_ref.shape[1], q_ref.shape[2]
    dh, cs, pad = DN_HEAD_DIM, DN_CHUNK, V7X_SUBLANES
    n_heads, n_chunks = gw // dh, tt // cs

    @pl.when(t == 0)
    def _():
        state_ref[...] = jnp.zeros_like(state_ref)

    def conv_silu(x_ref, xp_ref, cw_ref):
        @pl.when(t == 0)
        def _():
            xp_ref[0:pad, :] = jnp.zeros((pad, gw), F32)

        @pl.when(t > 0)
        def _():
            xp_ref[0:pad, :] = xp_ref[tt:tt + pad, :]

        xp_ref[pad:pad + tt, :] = x_ref[0]
        y = cw_ref[0:1, :] * xp_ref[pad - 3:pad - 3 + tt, :]
        for j in range(1, CONV_WIDTH):
            y = y + cw_ref[j:j + 1, :] * xp_ref[pad - 3 + j:pad - 3 + j + tt, :]
        return _silu(y)

    q_c, k_c, v_c = conv_silu(q_ref, xq_ref, cwq_ref), conv_silu(k_ref, xk_ref, cwk_ref), conv_silu(v_ref, xv_ref, cwv_ref)
    beta_all, g_all = _dn_gates(ba_ref[0], alog_ref[...], dtb_ref[...])
    per_head = {name: [] for name in ("q", "k", "v", "beta", "g")}
    for h in range(n_heads):
        lanes = slice(h * dh, (h + 1) * dh)
        head = grp * n_heads + h
        per_head["q"].append(_l2_normalize(q_c[:, lanes]) * (dh ** -0.5))
        per_head["k"].append(_l2_normalize(k_c[:, lanes]))
        per_head["v"].append(v_c[:, lanes])
        per_head["beta"].append(_head_column(beta_all, head))
        per_head["g"].append(_head_column(g_all, DN_HEADS + head))
    nb = n_heads * n_chunks
    stack = lambda name: jnp.concatenate(per_head[name], axis=0).reshape(nb, cs, -1)
    qc, kc, vc, beta, g = stack("q"), stack("k"), stack("v"), stack("beta"), stack("g")

    bmm = lambda a, b: lax.dot_general(a, b, (((2,), (1,)), ((0,), (0,))), preferred_element_type=F32)
    bmm_nt = lambda a, b: lax.dot_general(a, b, (((2,), (2,)), ((0,), (0,))), preferred_element_type=F32)
    low = jnp.broadcast_to(low_ref[...][None], (nb, cs, cs))
    row = lax.broadcasted_iota(jnp.int32, (nb, cs, cs), 1)
    col = lax.broadcasted_iota(jnp.int32, (nb, cs, cs), 2)
    parts = _split3(g * uo_ref[...][None])
    gfull = bmm(low, parts[0]) + bmm(low, parts[1]) + bmm(low, parts[2])
    decay = jnp.exp(gfull[:, :, 0:cs])
    g_col = gfull[:, :, 2 * cs:]
    e_g = jnp.exp(g_col)
    g_last = g_col[:, cs - 1:cs, :]
    kb = kc.astype(BF16)
    a_mat = jnp.where(col < row, bmm_nt(kb, kb) * decay, 0.0) * beta
    p_mat = jnp.where(col <= row, bmm_nt(qc.astype(BF16), kb) * decay, 0.0).astype(BF16)
    rhs = jnp.concatenate([vc * beta, kc * (beta * e_g)], axis=2)
    t_off, x = -a_mat, a_mat
    for _ in range(int(math.log2(cs)) - 1):
        xb = x.astype(BF16)
        x = bmm(xb, xb)
        t_off = t_off + x + bmm(t_off.astype(BF16), x.astype(BF16))
    sol = rhs + bmm(t_off.astype(BF16), rhs.astype(BF16))
    uv, wk = sol[:, :, 0:dh], sol[:, :, dh:].astype(BF16)
    qg = (qc * e_g).astype(BF16)
    kg = (kc * jnp.exp(g_last - g_col)).astype(BF16)
    g_end = jnp.exp(g_last)

    states = [state_ref[h] for h in range(n_heads)]
    for c in range(n_chunks):
        rows = slice(c * cs, (c + 1) * cs)
        for h in range(n_heads):
            i = h * n_chunks + c
            sb = states[h].astype(BF16)
            ws = _dot(jnp.concatenate([wk[i], qg[i]], axis=0), sb)
            ub = (uv[i] - ws[0:cs]).astype(BF16)
            o = ws[cs:] + _dot(p_mat[i], ub)
            states[h] = g_end[i] * states[h] + _dot_tn(kg[i], ub)
            lanes = slice(h * dh, (h + 1) * dh)
            o_ref[0, rows, lanes] = _rms_normalize(o, gain_ref[...]) * _silu(z_ref[0, rows, lanes])
    for h in range(n_heads):
        state_ref[h] = states[h]

    @pl.when(t == pl.num_programs(2) - 1)
    def _():
        for h in range(n_heads):
            s_ref[0, h] = states[h]


def _dn_prompt(qkv, z, ba, p):
    bsz, t_len, _ = qkv.shape
    tt = min(DN_TIME_TILE, t_len)
    dh = DN_HEAD_DIM
    gw = DN_GROUP * dh
    groups = DN_HEADS // DN_GROUP
    lane_blk = lambda off: pl.BlockSpec((1, tt, gw), lambda b, g, t: (b, t, off + g))
    cw_blk = lambda off: pl.BlockSpec((CONV_WIDTH, gw), lambda b, g, t: (0, off + g))
    const = lambda shape: pl.BlockSpec(shape, lambda b, g, t: (0,) * len(shape), pipeline_mode=pl.Buffered(1))
    lanes_row = const((1, V7X_LANES))
    return pl.pallas_call(
        _dn_prompt_kernel,
        grid=(bsz, groups, t_len // tt),
        in_specs=[lanes_row, lanes_row, lane_blk(0), lane_blk(groups), lane_blk(2 * groups), lane_blk(0),
                  pl.BlockSpec((1, tt, V7X_LANES), lambda b, g, t: (b, t, 0)),
                  cw_blk(0), cw_blk(groups), cw_blk(2 * groups),
                  const((1, dh)), const((DN_CHUNK, DN_CHUNK)), const((DN_CHUNK, 4 * DN_CHUNK))],
        out_specs=[lane_blk(0), pl.BlockSpec((1, DN_GROUP, dh, dh), lambda b, g, t: (b, g, 0, 0))],
        out_shape=[jax.ShapeDtypeStruct((bsz, t_len, DN_WIDTH), F32),
                   jax.ShapeDtypeStruct((bsz, DN_HEADS, dh, dh), F32)],
        scratch_shapes=[pltpu.VMEM((tt + 2 * V7X_SUBLANES, gw), F32)] * 3 + [pltpu.VMEM((DN_GROUP, dh, dh), F32)],
        compiler_params=_params("arbitrary", "arbitrary", "arbitrary"),
        name="dn_prompt",
    )(p["dn_A_log"], p["dn_dt_bias"], qkv, qkv, qkv, z, ba, p["dn_conv_w"], p["dn_conv_w"], p["dn_conv_w"],
      p["dn_o_gain"], p["dn_low"], p["dn_uo"])


def _dn_sample_kernel(alog_ref, dtb_ref, qkv_ref, buf_ref, z_ref, ba_ref, s0_ref, cw_ref, gain_ref,
                      o_ref, tail_ref, s_ref):
    t_len = qkv_ref.shape[1]
    dh = DN_HEAD_DIM
    xp = jnp.concatenate([buf_ref[0], qkv_ref[0]], axis=0)
    y = cw_ref[0:1, :] * xp[0:t_len]
    for j in range(1, CONV_WIDTH):
        y = y + cw_ref[j:j + 1, :] * xp[j:j + t_len]
    y = _silu(y)
    tail_ref[0] = xp[t_len:t_len + CONV_WIDTH - 1]
    beta_all, g_all = _dn_gates(ba_ref[0], alog_ref[...], dtb_ref[...])
    pad_rows = jnp.zeros((V7X_SUBLANES - t_len, dh), F32)
    ks, bases, grams = [], [], []
    for h in range(DN_HEADS):
        q = _l2_normalize(y[:, h * dh:(h + 1) * dh]) * (dh ** -0.5)
        k = _l2_normalize(y[:, DN_WIDTH + h * dh:DN_WIDTH + (h + 1) * dh])
        kq = jnp.concatenate([k, q], axis=0).astype(BF16)
        ks.append(k)
        bases.append(_dot(kq, s0_ref[0, h].astype(BF16)))
        grams.append(_dot_nt(kq, kq))
    updates = []
    for h in range(DN_HEADS):
        k, base, gram = ks[h], bases[h], grams[h]
        v = y[:, 2 * DN_WIDTH + h * dh:2 * DN_WIDTH + (h + 1) * dh]
        beta, g = _head_column(beta_all, h), _head_column(g_all, DN_HEADS + h)
        g_cum = [g[0:1, :]]
        for t in range(1, t_len):
            g_cum.append(g_cum[-1] + g[t:t + 1, :])
        us, outs = [], []
        for t in range(t_len):
            seen = jnp.exp(g_cum[t]) * base[t:t + 1, :]
            out = jnp.exp(g_cum[t]) * base[t_len + t:t_len + t + 1, :]
            for s in range(t):
                decay = jnp.exp(g_cum[t] - g_cum[s])
                seen = seen + (decay * gram[t:t + 1, s:s + 1]) * us[s]
                out = out + (decay * gram[t_len + t:t_len + t + 1, s:s + 1]) * us[s]
            u = beta[t:t + 1, :] * (v[t:t + 1, :] - seen)
            us.append(u)
            outs.append(out + gram[t_len + t:t_len + t + 1, t:t + 1] * u)
        g_end = g_cum[-1]
        kg = jnp.concatenate([k[t:t + 1, :] * jnp.exp(g_end - g_cum[t]) for t in range(t_len)] + [pad_rows], axis=0)
        u_all = jnp.concatenate(us + [pad_rows], axis=0)
        updates.append((jnp.exp(g_end), kg.astype(BF16), u_all.astype(BF16)))
        o = jnp.concatenate(outs, axis=0)
        o_ref[0, :, h * dh:(h + 1) * dh] = _rms_normalize(o, gain_ref[...]) * _silu(z_ref[0, :, h * dh:(h + 1) * dh])
    for h, (keep, kg, u_all) in enumerate(updates):
        s_ref[0, h] = keep * s0_ref[0, h] + _dot_tn(kg, u_all)


def _dn_sample(qkv, buf, z, ba, s0, p):
    bsz, t_len, _ = qkv.shape
    dh = DN_HEAD_DIM
    per_b = lambda *shape: pl.BlockSpec((1,) + shape, lambda b: (b,) + (0,) * len(shape))
    const = lambda shape: pl.BlockSpec(shape, lambda b: (0,) * len(shape), pipeline_mode=pl.Buffered(1))
    lanes_row = const((1, V7X_LANES))
    return pl.pallas_call(
        _dn_sample_kernel,
        grid=(bsz,),
        in_specs=[lanes_row, lanes_row, per_b(t_len, 3 * DN_WIDTH), per_b(CONV_WIDTH - 1, 3 * DN_WIDTH), per_b(t_len, DN_WIDTH),
                  per_b(t_len, V7X_LANES), per_b(DN_HEADS, dh, dh), const((CONV_WIDTH, 3 * DN_WIDTH)), const((1, dh))],
        out_specs=[per_b(t_len, DN_WIDTH), per_b(CONV_WIDTH - 1, 3 * DN_WIDTH), per_b(DN_HEADS, dh, dh)],
        out_shape=[jax.ShapeDtypeStruct((bsz, t_len, DN_WIDTH), F32),
                   jax.ShapeDtypeStruct((bsz, CONV_WIDTH - 1, 3 * DN_WIDTH), F32),
                   jax.ShapeDtypeStruct((bsz, DN_HEADS, dh, dh), F32)],
        compiler_params=_params("arbitrary"),
        name="dn_sample",
    )(p["dn_A_log"], p["dn_dt_bias"], qkv, buf, z, ba, s0, p["dn_conv_w"], p["dn_o_gain"])


def _block_diag(w):
    n, c, d = w.shape
    eye = jnp.eye(n, dtype=w.dtype)
    return (eye[:, None, :, None] * w[:, :, None, :]).reshape(n * c, n * d)


def _prepare_weights(d):
    p = {}
    for name in ("w_ffn1_in", "w_ffn1_out", "w_ffn2_in", "w_ffn2_out"):
        p[name] = d[name].astype(BF16)
    p["w_in_even"] = d["w_in_even"][0].astype(BF16)
    p["w_out_even"] = d["w_out_even"][0].astype(BF16)
    p["ones_bd"] = _block_diag(jnp.ones((SB_HEADS, SB_HEAD_DIM, SB_HEAD_DIM), BF16))
    p["q_gain"] = jnp.tile(d["sb_q_gain"][0], SB_HEADS).reshape(1, SB_WIDTH)
    p["k_gain"] = jnp.tile(d["sb_k_gain"][0], SB_HEADS).reshape(1, SB_WIDTH)
    p["lru_conv_w"] = d["lru_conv_w"][0]
    p["lru_conv_b"] = d["lru_conv_b"][0].reshape(1, LRU_WIDTH)
    p["lru_wa"] = _block_diag(d["lru_w_a"][0]).astype(BF16)
    p["lru_wi"] = _block_diag(d["lru_w_i"][0]).astype(BF16)
    p["lru_ba"] = d["lru_b_a"][0].reshape(1, LRU_WIDTH)
    p["lru_bi"] = d["lru_b_i"][0].reshape(1, LRU_WIDTH)
    p["lru_lambda"] = d["lru_lambda"][0].reshape(1, LRU_WIDTH)
    w_odd = d["w_in_odd"][0]
    p["w_qkvz"] = w_odd[:, :4 * DN_WIDTH].astype(BF16)
    p["w_ba"] = jnp.pad(w_odd[:, 4 * DN_WIDTH:], ((0, 0), (0, V7X_LANES - 2 * DN_HEADS))).astype(BF16)
    p["w_out_odd"] = d["w_out_odd"][0].astype(BF16)
    p["dn_conv_w"] = d["dn_conv_w"][0]
    decay_lanes = lambda a: jnp.pad(a, (DN_HEADS, V7X_LANES - 2 * DN_HEADS)).reshape(1, V7X_LANES)
    p["dn_A_log"] = decay_lanes(d["dn_A_log"][0])
    p["dn_dt_bias"] = decay_lanes(d["dn_dt_bias"][0])
    p["dn_o_gain"] = d["dn_o_gain"][0].reshape(1, DN_HEAD_DIM)
    idx = jnp.arange(DN_CHUNK)
    p["dn_low"] = (idx[None, :] <= idx[:, None]).astype(BF16)
    later = (idx[:, None] > idx[None, :]).astype(F32)
    p["dn_uo"] = jnp.concatenate([later, jnp.zeros((DN_CHUNK, DN_CHUNK), F32), jnp.ones((DN_CHUNK, 2 * DN_CHUNK), F32)], axis=1)
    return p


def _even_layer_mixer(x, group, d, p):
    kind, bsz, t_len = group
    q, k, v, xr, xg = _even_in(x, d["norm_mix"][0], p["w_in_even"], p["ones_bd"], p["q_gain"], p["k_gain"])
    seq = lambda a: a.reshape(bsz, t_len, a.shape[-1])
    if kind == "prompt":
        attn = _sb_prompt(seq(q), seq(k), seq(v), d["sb_bias"][0], _tri_suffix(ATT_TILE))
        rec, tail, h_last = _lru_prompt(seq(xr), seq(xg), p)
        h_last = h_last[:, 0]
    else:
        bias_rows = jnp.broadcast_to(jnp.tile(d["sb_bias"][0], t_len)[:, None], (t_len * SB_HEADS, PAGE_SIZE))
        cache_k = jnp.transpose(d["cache_k"][0], (0, 2, 3, 1))
        cache_v = jnp.transpose(d["cache_v"][0], (0, 2, 3, 1))
        attn = _sb_sample(seq(q), seq(k), seq(v), cache_k, cache_v, d["page_table"], bias_rows, _tri_suffix(PAGE_SIZE))
        tm = lambda a: jnp.swapaxes(a, 0, 1)
        rec, tail, h_last = _lru_sample(tm(seq(xr)), tm(seq(xg)), tm(d["state_lru_conv"][0]), d["state_lru_h"][0], p)
        rec, tail = tm(rec), tm(tail)
    x = _out_proj([attn.reshape(-1, SB_WIDTH), rec.reshape(-1, LRU_WIDTH)], p["w_out_even"], x)
    heads = lambda a: a.reshape(bsz, t_len, SB_HEADS, SB_HEAD_DIM)
    return x, (heads(k), heads(v), tail, h_last)


def _odd_layer_mixer(x, group, d, p):
    kind, bsz, t_len = group
    qkv, z, ba = _odd_in(x, d["norm_mix"][1], p["w_qkvz"], p["w_ba"])
    seq = lambda a: a.reshape(bsz, t_len, a.shape[-1])
    if kind == "prompt":
        o, state = _dn_prompt(seq(qkv), seq(z), seq(ba), p)
        tail = seq(qkv)[:, t_len - (CONV_WIDTH - 1):, :]
    else:
        o, tail, state = _dn_sample(seq(qkv), d["state_dn_conv"][0], seq(z), seq(ba), d["state_dn_S"][0], p)
    x = _out_proj([o.reshape(-1, DN_WIDTH)], p["w_out_odd"], x)
    return x, (tail, state)


def kernel(x_prompt, x_sample, cache_k, cache_v, state_lru_conv, state_lru_h, state_dn_conv, state_dn_S, page_table, norm_ffn1, w_ffn1_in, w_ffn1_out, norm_mix, norm_ffn2, w_ffn2_in, w_ffn2_out, w_in_even, sb_q_gain, sb_k_gain, sb_bias, lru_conv_w, lru_conv_b, lru_w_a, lru_b_a, lru_w_i, lru_b_i, lru_lambda, w_out_even, w_in_odd, dn_conv_w, dn_A_log, dn_dt_bias, dn_o_gain, w_out_odd):
    d = dict(locals())
    assert norm_ffn1.shape[0] == 2 and w_in_even.shape[0] == 1 and w_in_odd.shape[0] == 1, "two-layer trunk only"
    p = _prepare_weights(d)
    results = []
    for kind, x in (("prompt", x_prompt), ("sample", x_sample)):
        bsz, t_len, _ = x.shape
        group = (kind, bsz, t_len)
        x = x.reshape(bsz * t_len, D_MODEL)
        x = _ffn(x, norm_ffn1[0], p["w_ffn1_in"][0], p["w_ffn1_out"][0])
        x, even_state = _even_layer_mixer(x, group, d, p)
        x = _ffn(x, norm_ffn2[0], p["w_ffn2_in"][0], p["w_ffn2_out"][0])
        x = _ffn(x, norm_ffn1[1], p["w_ffn1_in"][1], p["w_ffn1_out"][1])
        x, odd_state = _odd_layer_mixer(x, group, d, p)
        x = _ffn(x, norm_ffn2[1], p["w_ffn2_in"][1], p["w_ffn2_out"][1])
        results.append((x.reshape(bsz, t_len, D_MODEL),) + tuple(s[None] for s in even_state + odd_state))
    (yp, *prompt_state), (ys, *sample_state) = results
    return (yp, ys, *prompt_state, *sample_state)
```

```python
import functools
import math

import jax
import jax.numpy as jnp
from jax import lax
from jax.experimental import pallas as pl
from jax.experimental.pallas import tpu as pltpu

F32 = jnp.float32
BF16 = jnp.bfloat16

V7X_LANES = 128
V7X_SUBLANES = 8
V7X_VMEM_LIMIT_BYTES = 56 * 1024 * 1024

D_MODEL = 1024
D_FF = 2 * D_MODEL
SB_HEADS = 8
SB_HEAD_DIM = 64
SB_WIDTH = SB_HEADS * SB_HEAD_DIM
LRU_WIDTH = 512
LRU_BLOCKS = 8
LRU_C = 8.0
CONV_WIDTH = 4
DN_HEADS = 8
DN_HEAD_DIM = 128
DN_WIDTH = DN_HEADS * DN_HEAD_DIM
DN_CHUNK = 64
PAGE_SIZE = 128
EPS = 1e-6

TOKEN_TILE = 512
FF_CHUNK = 512
ATT_TILE = 256
LRU_TIME_TILE = 512


def _params(*sem):
    return pltpu.CompilerParams(dimension_semantics=sem, vmem_limit_bytes=V7X_VMEM_LIMIT_BYTES)


def _resident(shape):
    nd = len(shape)
    return pl.BlockSpec(shape, lambda *_: (0,) * nd, pipeline_mode=pl.Buffered(1))


def _rms_normalize(x, gain):
    ms = jnp.mean(x * x, axis=-1, keepdims=True)
    return x * lax.rsqrt(ms + EPS) * gain


LOG2E = 1.0 / math.log(2.0)


def _softplus(x):
    return jnp.maximum(x, 0.0) + jnp.log(1.0 + jnp.exp2(jnp.abs(x) * (-LOG2E)))


def _split2(x):
    hi = x.astype(BF16)
    lo = (x - hi.astype(F32)).astype(BF16)
    return hi, lo


def _split3(x):
    hi = x.astype(BF16)
    r1 = x - hi.astype(F32)
    mid = r1.astype(BF16)
    lo = (r1 - mid.astype(F32)).astype(BF16)
    return hi, mid, lo


def _dot(a, b):
    return jnp.dot(a, b, preferred_element_type=F32)


def _dot_nt(a, b):
    return lax.dot_general(a, b, (((1,), (1,)), ((), ())), preferred_element_type=F32)


def _dot_tn(a, b):
    return lax.dot_general(a, b, (((0,), (0,)), ((), ())), preferred_element_type=F32)


def _ffn_kernel(x_ref, g_ref, win_ref, wout_ref, o_ref, act_ref):
    x = x_ref[...]
    xn = _rms_normalize(x, g_ref[...]).astype(BF16)
    for c in range(D_FF // FF_CHUNK):
        lo = c * FF_CHUNK
        gate = _dot(xn, win_ref[:, lo:lo + FF_CHUNK])
        up = _dot(xn, win_ref[:, D_FF + lo:D_FF + lo + FF_CHUNK])
        act_ref[:, lo:lo + FF_CHUNK] = (gate * jax.nn.sigmoid(gate) * up).astype(BF16)
    o_ref[...] = x + 0.5 * _dot(act_ref[...], wout_ref[...])


def _ffn(x, gain, w_in, w_out):
    n = x.shape[0]
    tm = min(TOKEN_TILE, n)
    return pl.pallas_call(
        _ffn_kernel,
        grid=(n // tm,),
        in_specs=[
            pl.BlockSpec((tm, D_MODEL), lambda i: (i, 0)),
            _resident((1, D_MODEL)),
            _resident((D_MODEL, 2 * D_FF)),
            _resident((D_FF, D_MODEL)),
        ],
        out_specs=pl.BlockSpec((tm, D_MODEL), lambda i: (i, 0)),
        out_shape=jax.ShapeDtypeStruct((n, D_MODEL), F32),
        scratch_shapes=[pltpu.VMEM((tm, D_FF), BF16)],
        compiler_params=_params("arbitrary"),
        name="ffn",
    )(x, gain.reshape(1, D_MODEL), w_in, w_out)


def _head_rms(x, ones_bd, gain):
    hi, lo = _split2(x * x)
    ms = (_dot(hi, ones_bd) + _dot(lo, ones_bd)) * (1.0 / SB_HEAD_DIM)
    return x * lax.rsqrt(ms + EPS) * gain


def _even_in_kernel(x_ref, g_ref, w_ref, ones_ref, qg_ref, kg_ref,
                    q_ref, k_ref, v_ref, xr_ref, xg_ref):
    xn = _rms_normalize(x_ref[...], g_ref[...]).astype(BF16)
    ones_bd = ones_ref[...]
    q = _dot(xn, w_ref[:, 0:SB_WIDTH])
    q_ref[...] = _head_rms(q, ones_bd, qg_ref[...])
    k = _dot(xn, w_ref[:, SB_WIDTH:2 * SB_WIDTH])
    k_ref[...] = _head_rms(k, ones_bd, kg_ref[...])
    v_ref[...] = _dot(xn, w_ref[:, 2 * SB_WIDTH:3 * SB_WIDTH])
    xr_ref[...] = _dot(xn, w_ref[:, 3 * SB_WIDTH:3 * SB_WIDTH + LRU_WIDTH])
    xg_ref[...] = _dot(xn, w_ref[:, 3 * SB_WIDTH + LRU_WIDTH:])


def _even_in(x, gain, w, ones_bd, q_gain, k_gain):
    n = x.shape[0]
    tm = min(TOKEN_TILE, n)
    width = w.shape[1]
    row = lambda i: (i, 0)
    out = jax.ShapeDtypeStruct((n, SB_WIDTH), F32)
    return pl.pallas_call(
        _even_in_kernel,
        grid=(n // tm,),
        in_specs=[
            pl.BlockSpec((tm, D_MODEL), row),
            _resident((1, D_MODEL)),
            _resident((D_MODEL, width)),
            _resident((SB_WIDTH, SB_WIDTH)),
            _resident((1, SB_WIDTH)),
            _resident((1, SB_WIDTH)),
        ],
        out_specs=[pl.BlockSpec((tm, SB_WIDTH), row)] * 5,
        out_shape=[out] * 5,
        compiler_params=_params("arbitrary"),
        name="even_in",
    )(x, gain.reshape(1, D_MODEL), w, ones_bd, q_gain, k_gain)


def _out_proj_kernel(*refs, n_parts):
    parts, (w_ref, x_ref, o_ref) = refs[:n_parts], refs[n_parts:]
    acc = x_ref[...]
    row = 0
    for p in parts:
        width = p.shape[-1]
        acc = acc + _dot(p[...].astype(BF16), w_ref[row:row + width, :])
        row += width
    o_ref[...] = acc


def _out_proj(parts, w, x):
    n = x.shape[0]
    tm = min(TOKEN_TILE, n)
    row = lambda i: (i, 0)
    return pl.pallas_call(
        functools.partial(_out_proj_kernel, n_parts=len(parts)),
        grid=(n // tm,),
        in_specs=[pl.BlockSpec((tm, p.shape[1]), row) for p in parts]
        + [_resident(w.shape), pl.BlockSpec((tm, D_MODEL), row)],
        out_specs=pl.BlockSpec((tm, D_MODEL), row),
        out_shape=jax.ShapeDtypeStruct((n, D_MODEL), F32),
        compiler_params=_params("arbitrary"),
        name="out_proj",
    )(*parts, w, x)


def _gelu_tanh(x):
    cdf = 0.5 * (1.0 + jnp.tanh(math.sqrt(2.0 / math.pi) * (x + 0.044715 * (x * x * x))))
    return x * cdf


def _lru_coeffs(xc, wa_ref, ba_ref, wi_ref, bi_ref, sp_lam):
    xb = xc.astype(BF16)
    r = jax.nn.sigmoid(_dot(xb, wa_ref[...]) + ba_ref[...])
    i = jax.nn.sigmoid(_dot(xb, wi_ref[...]) + bi_ref[...])
    log_a = -LRU_C * r * sp_lam
    a = jnp.exp(log_a)
    b = jnp.sqrt(-jnp.tanh(log_a) * (a * a + 1.0)) * (i * xc)
    return a, b


def _lru_prompt_kernel(xr_ref, xg_ref, cw_ref, cb_ref, wa_ref, ba_ref, wi_ref, bi_ref, lam_ref,
                       rec_ref, tail_ref, hlast_ref, xp_ref, a_ref, b_ref, h_ref, hc_ref):
    t = pl.program_id(1)
    tt = xr_ref.shape[1]
    pad = V7X_SUBLANES

    @pl.when(t == 0)
    def _():
        xp_ref[0:pad, :] = jnp.zeros((pad, LRU_WIDTH), F32)
        hc_ref[...] = jnp.zeros_like(hc_ref)

    @pl.when(t > 0)
    def _():
        xp_ref[0:pad, :] = xp_ref[tt:tt + pad, :]

    xp_ref[pad:pad + tt, :] = xr_ref[0]
    xc = cw_ref[0:1, :] * xp_ref[pad - 3:pad - 3 + tt, :]
    for j in range(1, CONV_WIDTH):
        xc = xc + cw_ref[j:j + 1, :] * xp_ref[pad - 3 + j:pad - 3 + j + tt, :]
    xc = xc + cb_ref[...]
    a, b = _lru_coeffs(xc, wa_ref, ba_ref, wi_ref, bi_ref, _softplus(-lam_ref[...]))
    a_ref[...] = a
    b_ref[...] = b

    row = lax.broadcasted_iota(jnp.int32, (V7X_SUBLANES, LRU_WIDTH), 0)

    def group(j, h):
        base = pl.multiple_of(j * V7X_SUBLANES, V7X_SUBLANES)
        ag = a_ref[pl.ds(base, V7X_SUBLANES), :]
        bg = b_ref[pl.ds(base, V7X_SUBLANES), :]
        for d in (1, 2, 4):
            keep = row >= d
            a_prev = jnp.where(keep, pltpu.roll(ag, d, axis=0), 1.0)
            b_prev = jnp.where(keep, pltpu.roll(bg, d, axis=0), 0.0)
            bg = ag * b_prev + bg
            ag = ag * a_prev
        hg = ag * h + bg
        h_ref[pl.ds(base, V7X_SUBLANES), :] = hg
        return hg[V7X_SUBLANES - 1:V7X_SUBLANES, :]

    h_end = lax.fori_loop(0, tt // V7X_SUBLANES, group, hc_ref[...])
    hc_ref[...] = h_end
    rec_ref[0] = h_ref[...] * _gelu_tanh(xg_ref[0])

    @pl.when(t == pl.num_programs(1) - 1)
    def _():
        tail_ref[0] = xp_ref[pad + tt - 3:pad + tt, :]
        hlast_ref[0] = h_end


def _lru_prompt(xr, xg, p):
    bsz, t_len, w = xr.shape
    tt = min(LRU_TIME_TILE, t_len)
    tile = pl.BlockSpec((1, tt, w), lambda b, t: (b, t, 0))
    per_b = lambda rows: pl.BlockSpec((1, rows, w), lambda b, t: (b, 0, 0))
    return pl.pallas_call(
        _lru_prompt_kernel,
        grid=(bsz, t_len // tt),
        in_specs=[tile, tile, _resident((CONV_WIDTH, w)), _resident((1, w)), _resident((w, w)),
                  _resident((1, w)), _resident((w, w)), _resident((1, w)), _resident((1, w))],
        out_specs=[tile, per_b(CONV_WIDTH - 1), per_b(1)],
        out_shape=[jax.ShapeDtypeStruct((bsz, t_len, w), F32),
                   jax.ShapeDtypeStruct((bsz, CONV_WIDTH - 1, w), F32),
                   jax.ShapeDtypeStruct((bsz, 1, w), F32)],
        scratch_shapes=[pltpu.VMEM((tt + 2 * V7X_SUBLANES, w), F32), pltpu.VMEM((tt, w), F32),
                        pltpu.VMEM((tt, w), F32), pltpu.VMEM((tt, w), F32), pltpu.VMEM((1, w), F32)],
        compiler_params=_params("arbitrary", "arbitrary"),
        name="lru_prompt",
    )(xr, xg, p["lru_conv_w"], p["lru_conv_b"], p["lru_wa"], p["lru_ba"], p["lru_wi"], p["lru_bi"],
      p["lru_lambda"])


def _lru_sample_kernel(xr_ref, xg_ref, buf_ref, h0_ref, cw_ref, cb_ref, wa_ref, ba_ref, wi_ref, bi_ref,
                       lam_ref, rec_ref, tail_ref, hlast_ref):
    t_len = xr_ref.shape[0]
    xp = [buf_ref[j] for j in range(CONV_WIDTH - 1)] + [xr_ref[j] for j in range(t_len)]
    sp_lam = _softplus(-lam_ref[...])
    h = h0_ref[...]
    for t in range(t_len):
        xc = cw_ref[0:1, :] * xp[t]
        for j in range(1, CONV_WIDTH):
            xc = xc + cw_ref[j:j + 1, :] * xp[t + j]
        xc = xc + cb_ref[...]
        a, b = _lru_coeffs(xc, wa_ref, ba_ref, wi_ref, bi_ref, sp_lam)
        h = a * h + b
        rec_ref[t] = h * _gelu_tanh(xg_ref[t])
    for j in range(CONV_WIDTH - 1):
        tail_ref[j] = xp[t_len + j]
    hlast_ref[...] = h


def _lru_sample(xr, xg, buf, h0, p):
    t_len, bsz, w = xr.shape
    full = lambda shape: pl.BlockSpec(shape, lambda i: (0,) * len(shape))
    return pl.pallas_call(
        _lru_sample_kernel,
        grid=(1,),
        in_specs=[full((t_len, bsz, w)), full((t_len, bsz, w)), full((CONV_WIDTH - 1, bsz, w)), full((bsz, w)),
                  full((CONV_WIDTH, w)), full((1, w)), full((w, w)), full((1, w)), full((w, w)), full((1, w)),
                  full((1, w))],
        out_specs=[full((t_len, bsz, w)), full((CONV_WIDTH - 1, bsz, w)), full((bsz, w))],
        out_shape=[jax.ShapeDtypeStruct((t_len, bsz, w), F32),
                   jax.ShapeDtypeStruct((CONV_WIDTH - 1, bsz, w), F32),
                   jax.ShapeDtypeStruct((bsz, w), F32)],
        compiler_params=_params("arbitrary"),
        name="lru_sample",
    )(xr, xg, buf, h0, p["lru_conv_w"], p["lru_conv_b"], p["lru_wa"], p["lru_ba"], p["lru_wi"], p["lru_bi"],
      p["lru_lambda"])


def _sb_chains(scores, masks, bias, tri, carry, one_dot):
    rows = scores[0].shape[0]
    zs, parts = [], []
    for s, m in zip(scores, masks):
        z = s + bias
        sp = _softplus(z)
        if m is not None:
            sp = jnp.where(m, sp, 0.0)
        zs.append(z)
        parts.append(jnp.concatenate(_split2(sp), axis=1))
    if one_dot:
        cum = _dot(jnp.concatenate(parts, axis=0), tri)
        cums = [cum[i * rows:(i + 1) * rows] for i in range(len(parts))]
    else:
        cums = [_dot(part, tri) for part in parts]
    ws = []
    for z, c, m in zip(zs, cums, masks):
        w = jnp.exp(z - c - carry)
        if m is not None:
            w = jnp.where(m, w, 0.0)
        ws.append(w.astype(BF16))
        carry = carry + c[:, 0:1]
    return carry, ws


def _sb_prompt_kernel(bias_ref, q_ref, k_ref, v_ref, tri_ref, o_ref):
    hp, qi = pl.program_id(1), pl.program_id(2)
    tq = q_ref.shape[1]
    kw = tq // 2
    q = q_ref[0] * (SB_HEAD_DIM ** -0.5)
    lane = lax.broadcasted_iota(jnp.int32, (tq, 2 * SB_HEAD_DIM), 1)
    q2 = jnp.concatenate([jnp.where(lane < SB_HEAD_DIM, q, 0.0), jnp.where(lane < SB_HEAD_DIM, 0.0, q)], axis=0)
    q2 = q2.astype(BF16)
    first_head = lax.broadcasted_iota(jnp.int32, (2 * tq, 1), 0) < tq
    bias = jnp.where(first_head, bias_ref[hp * 2], bias_ref[hp * 2 + 1])
    tri = tri_ref[...]

    def pair_step(j, masks, carry, acc):
        start = pl.multiple_of(j * tq, tq)
        newer, older = pl.ds(start + kw, kw), pl.ds(start, kw)
        scores = [_dot_nt(q2, k_ref[0, newer, :].astype(BF16)), _dot_nt(q2, k_ref[0, older, :].astype(BF16))]
        carry, ws = _sb_chains(scores, masks, bias, tri, carry, one_dot=False)
        acc = acc + _dot(ws[0], v_ref[0, newer, :].astype(BF16)) + _dot(ws[1], v_ref[0, older, :].astype(BF16))
        return carry, acc

    q_pos = lax.broadcasted_iota(jnp.int32, (2 * tq, kw), 0) & (tq - 1)
    k_pos = lax.broadcasted_iota(jnp.int32, (2 * tq, kw), 1)
    state = (jnp.zeros((2 * tq, 1), F32), jnp.zeros((2 * tq, 2 * SB_HEAD_DIM), F32))
    state = pair_step(qi, [k_pos + kw < q_pos, k_pos < q_pos], *state)
    state = lax.fori_loop(0, qi, lambda it, st: pair_step(qi - 1 - it, [None, None], *st), state)
    acc = state[1]
    o_ref[0] = jnp.where(lane < SB_HEAD_DIM, acc[:tq], acc[tq:])


def _sb_prompt(q, k, v, bias, tri):
    bsz, t_len, width = q.shape
    tq = 2 * ATT_TILE
    assert t_len % tq == 0 and tq & (tq - 1) == 0
    pair = 2 * SB_HEAD_DIM
    tile = pl.BlockSpec((1, tq, pair), lambda b, hp, qi: (b, qi, hp))
    seq = pl.BlockSpec((1, t_len, pair), lambda b, hp, qi: (b, 0, hp))
    return pl.pallas_call(
        _sb_prompt_kernel,
        grid=(bsz, width // pair, t_len // tq),
        in_specs=[pl.BlockSpec(memory_space=pltpu.SMEM), tile, seq, seq,
                  pl.BlockSpec((tq, ATT_TILE), lambda b, hp, qi: (0, 0), pipeline_mode=pl.Buffered(1))],
        out_specs=tile,
        out_shape=jax.ShapeDtypeStruct((bsz, t_len, width), F32),
        compiler_params=_params("arbitrary", "arbitrary", "arbitrary"),
        name="sb_prompt",
    )(bias, q, k, v, tri)


def _sb_sample_kernel(pt_ref, q_ref, kn_ref, vn_ref, bias_ref, tri_ref, *refs, n_pages):
    del pt_ref
    k_pages, v_pages, o_ref = refs[:n_pages], refs[n_pages:2 * n_pages], refs[2 * n_pages]
    t_len, width = q_ref.shape[1], q_ref.shape[2]
    rows = t_len * SB_HEADS
    q = q_ref[0] * (SB_HEAD_DIM ** -0.5)
    qb = jnp.broadcast_to(q[:, None, :], (t_len, SB_HEADS, width)).reshape(rows, width)
    row = lax.broadcasted_iota(jnp.int32, (rows, width), 0)
    lane = lax.broadcasted_iota(jnp.int32, (rows, width), 1)
    own_head = (row & (SB_HEADS - 1)) == (lane >> int(math.log2(SB_HEAD_DIM)))
    qbd = jnp.where(own_head, qb, 0.0).astype(BF16)
    page = lambda ref: ref[...].reshape(width, PAGE_SIZE).astype(BF16)

    pad = jnp.zeros((PAGE_SIZE - t_len, width), F32)
    k_new = jnp.concatenate([kn_ref[0], pad], axis=0).astype(BF16)
    v_new = jnp.concatenate([vn_ref[0], pad], axis=0).astype(BF16)
    key = lax.broadcasted_iota(jnp.int32, (rows, PAGE_SIZE), 1)
    tok = lax.broadcasted_iota(jnp.int32, (rows, PAGE_SIZE), 0) >> int(math.log2(SB_HEADS))
    order = list(reversed(range(n_pages)))
    scores = [_dot_nt(qbd, k_new)] + [_dot(qbd, page(k_pages[p])) for p in order]
    masks = [key < tok] + [None] * n_pages
    _, ws = _sb_chains(scores, masks, bias_ref[...], tri_ref[...], jnp.zeros((rows, 1), F32), one_dot=True)
    acc = _dot(ws[0], v_new)
    for w, p in zip(ws[1:], order):
        acc = acc + _dot_nt(w, page(v_pages[p]))
    acc = jnp.where(own_head, acc, 0.0)
    o_ref[0] = jnp.sum(acc.reshape(t_len, SB_HEADS, width), axis=1)


def _sb_sample(q, k_new, v_new, cache_k, cache_v, page_table, bias_rows, tri):
    bsz, t_len, width = q.shape
    n_pages = page_table.shape[1]
    rows = t_len * SB_HEADS
    tok = pl.BlockSpec((1, t_len, width), lambda b, pt: (b, 0, 0))
    const = lambda shape: pl.BlockSpec(shape, lambda b, pt: (0, 0), pipeline_mode=pl.Buffered(1))
    pages = [pl.BlockSpec((None, SB_HEADS, SB_HEAD_DIM, PAGE_SIZE), lambda b, pt, j=j: (pt[b, j], 0, 0, 0))
             for j in range(n_pages)]
    grid_spec = pltpu.PrefetchScalarGridSpec(
        num_scalar_prefetch=1,
        grid=(bsz,),
        in_specs=[tok, tok, tok, const((rows, PAGE_SIZE)), const((2 * PAGE_SIZE, PAGE_SIZE))] + pages + pages,
        out_specs=tok,
    )
    return pl.pallas_call(
        functools.partial(_sb_sample_kernel, n_pages=n_pages),
        grid_spec=grid_spec,
        out_shape=jax.ShapeDtypeStruct((bsz, t_len, width), F32),
        compiler_params=_params("arbitrary"),
        name="sb_sample",
    )(page_table, q, k_new, v_new, bias_rows, tri, *([cache_k] * n_pages), *([cache_v] * n_pages))


def _tri_suffix(n):
    idx = jnp.arange(n)
    tri = (idx[:, None] >= idx[None, :]).astype(BF16)
    return jnp.concatenate([tri, tri], axis=0)


ODD_COL_CHUNK = 1024
DN_TIME_TILE = 1024
DN_GROUP = 2


def _odd_in_kernel(x_ref, g_ref, w_ref, wba_ref, qkv_ref, z_ref, ba_ref):
    xn = _rms_normalize(x_ref[...], g_ref[...]).astype(BF16)
    for c in range(3 * DN_WIDTH // ODD_COL_CHUNK):
        lo = c * ODD_COL_CHUNK
        qkv_ref[:, lo:lo + ODD_COL_CHUNK] = _dot(xn, w_ref[:, lo:lo + ODD_COL_CHUNK])
    z_ref[...] = _dot(xn, w_ref[:, 3 * DN_WIDTH:])
    ba_ref[...] = _dot(xn, wba_ref[...])


def _odd_in(x, gain, w_qkvz, w_ba):
    n = x.shape[0]
    tm = min(TOKEN_TILE, n)
    row = lambda i: (i, 0)
    return pl.pallas_call(
        _odd_in_kernel,
        grid=(n // tm,),
        in_specs=[pl.BlockSpec((tm, D_MODEL), row), _resident((1, D_MODEL)),
                  _resident(w_qkvz.shape), _resident(w_ba.shape)],
        out_specs=[pl.BlockSpec((tm, 3 * DN_WIDTH), row), pl.BlockSpec((tm, DN_WIDTH), row),
                   pl.BlockSpec((tm, V7X_LANES), row)],
        out_shape=[jax.ShapeDtypeStruct((n, 3 * DN_WIDTH), F32), jax.ShapeDtypeStruct((n, DN_WIDTH), F32),
                   jax.ShapeDtypeStruct((n, V7X_LANES), F32)],
        compiler_params=_params("arbitrary"),
        name="odd_in",
    )(x, gain.reshape(1, D_MODEL), w_qkvz, w_ba)


def _l2_normalize(x):
    return x * lax.rsqrt(jnp.sum(x * x, axis=-1, keepdims=True) + EPS)


def _silu(x):
    return x * jax.nn.sigmoid(x)


def _head_column(x, lane_index):
    lane = lax.broadcasted_iota(jnp.int32, x.shape, 1)
    return jnp.sum(jnp.where(lane == lane_index, x, 0.0), axis=-1, keepdims=True)


def _dn_gates(ba, alog_row, dtb_row):
    beta = jax.nn.sigmoid(ba)
    g = -jnp.exp(alog_row) * _softplus(ba + dtb_row)
    return beta, g


def _dn_prompt_kernel(alog_ref, dtb_ref, q_ref, k_ref, v_ref, z_ref, ba_ref, cwq_ref, cwk_ref, cwv_ref,
                      gain_ref, low_ref, uo_ref, o_ref, s_ref, xq_ref, xk_ref, xv_ref, state_ref):
    grp, t = pl.program_id(1), pl.program_id(2)
    tt, gw = q_ref.shape[1], q_ref.shape[2]
    dh, cs, pad = DN_HEAD_DIM, DN_CHUNK, V7X_SUBLANES
    n_heads, n_chunks = gw // dh, tt // cs

    @pl.when(t == 0)
    def _():
        state_ref[...] = jnp.zeros_like(state_ref)

    def conv_silu(x_ref, xp_ref, cw_ref):
        @pl.when(t == 0)
        def _():
            xp_ref[0:pad, :] = jnp.zeros((pad, gw), F32)

        @pl.when(t > 0)
        def _():
            xp_ref[0:pad, :] = xp_ref[tt:tt + pad, :]

        xp_ref[pad:pad + tt, :] = x_ref[0]
        y = cw_ref[0:1, :] * xp_ref[pad - 3:pad - 3 + tt, :]
        for j in range(1, CONV_WIDTH):
            y = y + cw_ref[j:j + 1, :] * xp_ref[pad - 3 + j:pad - 3 + j + tt, :]
        return _silu(y)

    q_c, k_c, v_c = conv_silu(q_ref, xq_ref, cwq_ref), conv_silu(k_ref, xk_ref, cwk_ref), conv_silu(v_ref, xv_ref, cwv_ref)
    beta_all, g_all = _dn_gates(ba_ref[0], alog_ref[...], dtb_ref[...])
    per_head = {name: [] for name in ("q", "k", "v", "beta", "g")}
    for h in range(n_heads):
        lanes = slice(h * dh, (h + 1) * dh)
        head = grp * n_heads + h
        per_head["q"].append(_l2_normalize(q_c[:, lanes]) * (dh ** -0.5))
        per_head["k"].append(_l2_normalize(k_c[:, lanes]))
        per_head["v"].append(v_c[:, lanes])
        per_head["beta"].append(_head_column(beta_all, head))
        per_head["g"].append(_head_column(g_all, DN_HEADS + head))
    nb = n_heads * n_chunks
    stack = lambda name: jnp.concatenate(per_head[name], axis=0).reshape(nb, cs, -1)
    qc, kc, vc, beta, g = stack("q"), stack("k"), stack("v"), stack("beta"), stack("g")

    bmm = lambda a, b: lax.dot_general(a, b, (((2,), (1,)), ((0,), (0,))), preferred_element_type=F32)
    bmm_nt = lambda a, b: lax.dot_general(a, b, (((2,), (2,)), ((0,), (0,))), preferred_element_type=F32)
    low = jnp.broadcast_to(low_ref[...][None], (nb, cs, cs))
    row = lax.broadcasted_iota(jnp.int32, (nb, cs, cs), 1)
    col = lax.broadcasted_iota(jnp.int32, (nb, cs, cs), 2)
    parts = _split3(g * uo_ref[...][None])
    gfull = bmm(low, parts[0]) + bmm(low, parts[1]) + bmm(low, parts[2])
    decay = jnp.exp(gfull[:, :, 0:cs])
    g_col = gfull[:, :, 2 * cs:]
    e_g = jnp.exp(g_col)
    g_last = g_col[:, cs - 1:cs, :]
    kb = kc.astype(BF16)
    a_mat = jnp.where(col < row, bmm_nt(kb, kb) * decay, 0.0) * beta
    p_mat = jnp.where(col <= row, bmm_nt(qc.astype(BF16), kb) * decay, 0.0).astype(BF16)
    rhs = jnp.concatenate([vc * beta, kc * (beta * e_g)], axis=2)
    t_off, x = -a_mat, a_mat
    for _ in range(int(math.log2(cs)) - 1):
        xb = x.astype(BF16)
        x = bmm(xb, xb)
        t_off = t_off + x + bmm(t_off.astype(BF16), x.astype(BF16))
    sol = rhs + bmm(t_off.astype(BF16), rhs.astype(BF16))
    uv, wk = sol[:, :, 0:dh], sol[:, :, dh:].astype(BF16)
    qg = (qc * e_g).astype(BF16)
    kg = (kc * jnp.exp(g_last - g_col)).astype(BF16)
    g_end = jnp.exp(g_last)

    bmm_tn = lambda a, b: lax.dot_general(a, b, (((1,), (1,)), ((0,), (0,))), preferred_element_type=F32)
    mix = bmm_tn(kg, wk).astype(BF16)
    fresh = bmm_tn(kg, uv.astype(BF16))
    states = [state_ref[h] for h in range(n_heads)]
    entering = [[None] * n_chunks for _ in range(n_heads)]
    for c in range(n_chunks):
        for h in range(n_heads):
            i = h * n_chunks + c
            sb = states[h].astype(BF16)
            entering[h][c] = sb
            states[h] = g_end[i] * states[h] - _dot(mix[i], sb) + fresh[i]
    for h in range(n_heads):
        state_ref[h] = states[h]
    s_in = jnp.stack([entering[h][c] for h in range(n_heads) for c in range(n_chunks)], axis=0)
    ws = bmm(jnp.concatenate([wk, qg], axis=1), s_in)
    ub = (uv - ws[:, 0:cs, :]).astype(BF16)
    o = ws[:, cs:, :] + bmm(p_mat, ub)
    for h in range(n_heads):
        lanes = slice(h * dh, (h + 1) * dh)
        o_h = o[h * n_chunks:(h + 1) * n_chunks].reshape(tt, dh)
        o_ref[0, :, lanes] = _rms_normalize(o_h, gain_ref[...]) * _silu(z_ref[0, :, lanes])

    @pl.when(t == pl.num_programs(2) - 1)
    def _():
        for h in range(n_heads):
            s_ref[0, h] = states[h]


def _dn_prompt(qkv, z, ba, p):
    bsz, t_len, _ = qkv.shape
    tt = min(DN_TIME_TILE, t_len)
    dh = DN_HEAD_DIM
    gw = DN_GROUP * dh
    groups = DN_HEADS // DN_GROUP
    lane_blk = lambda off: pl.BlockSpec((1, tt, gw), lambda b, g, t: (b, t, off + g))
    cw_blk = lambda off: pl.BlockSpec((CONV_WIDTH, gw), lambda b, g, t: (0, off + g))
    const = lambda shape: pl.BlockSpec(shape, lambda b, g, t: (0,) * len(shape), pipeline_mode=pl.Buffered(1))
    lanes_row = const((1, V7X_LANES))
    return pl.pallas_call(
        _dn_prompt_kernel,
        grid=(bsz, groups, t_len // tt),
        in_specs=[lanes_row, lanes_row, lane_blk(0), lane_blk(groups), lane_blk(2 * groups), lane_blk(0),
                  pl.BlockSpec((1, tt, V7X_LANES), lambda b, g, t: (b, t, 0)),
                  cw_blk(0), cw_blk(groups), cw_blk(2 * groups),
                  const((1, dh)), const((DN_CHUNK, DN_CHUNK)), const((DN_CHUNK, 4 * DN_CHUNK))],
        out_specs=[lane_blk(0), pl.BlockSpec((1, DN_GROUP, dh, dh), lambda b, g, t: (b, g, 0, 0))],
        out_shape=[jax.ShapeDtypeStruct((bsz, t_len, DN_WIDTH), F32),
                   jax.ShapeDtypeStruct((bsz, DN_HEADS, dh, dh), F32)],
        scratch_shapes=[pltpu.VMEM((tt + 2 * V7X_SUBLANES, gw), F32)] * 3 + [pltpu.VMEM((DN_GROUP, dh, dh), F32)],
        compiler_params=_params("arbitrary", "arbitrary", "arbitrary"),
        name="dn_prompt",
    )(p["dn_A_log"], p["dn_dt_bias"], qkv, qkv, qkv, z, ba, p["dn_conv_w"], p["dn_conv_w"], p["dn_conv_w"],
      p["dn_o_gain"], p["dn_low"], p["dn_uo"])


def _dn_sample_kernel(alog_ref, dtb_ref, qkv_ref, buf_ref, z_ref, ba_ref, s0_ref, cw_ref, gain_ref,
                      o_ref, tail_ref, s_ref):
    t_len = qkv_ref.shape[1]
    dh = DN_HEAD_DIM
    xp = jnp.concatenate([buf_ref[0], qkv_ref[0]], axis=0)
    y = cw_ref[0:1, :] * xp[0:t_len]
    for j in range(1, CONV_WIDTH):
        y = y + cw_ref[j:j + 1, :] * xp[j:j + t_len]
    y = _silu(y)
    tail_ref[0] = xp[t_len:t_len + CONV_WIDTH - 1]
    beta_all, g_all = _dn_gates(ba_ref[0], alog_ref[...], dtb_ref[...])
    pad_rows = jnp.zeros((V7X_SUBLANES - t_len, dh), F32)
    ks, bases, grams = [], [], []
    for h in range(DN_HEADS):
        q = _l2_normalize(y[:, h * dh:(h + 1) * dh]) * (dh ** -0.5)
        k = _l2_normalize(y[:, DN_WIDTH + h * dh:DN_WIDTH + (h + 1) * dh])
        kq = jnp.concatenate([k, q], axis=0).astype(BF16)
        ks.append(k)
        bases.append(_dot(kq, s0_ref[0, h].astype(BF16)))
        grams.append(_dot_nt(kq, kq))
    updates = []
    for h in range(DN_HEADS):
        k, base, gram = ks[h], bases[h], grams[h]
        v = y[:, 2 * DN_WIDTH + h * dh:2 * DN_WIDTH + (h + 1) * dh]
        beta, g = _head_column(beta_all, h), _head_column(g_all, DN_HEADS + h)
        g_cum = [g[0:1, :]]
        for t in range(1, t_len):
            g_cum.append(g_cum[-1] + g[t:t + 1, :])
        us, outs = [], []
        for t in range(t_len):
            seen = jnp.exp(g_cum[t]) * base[t:t + 1, :]
            out = jnp.exp(g_cum[t]) * base[t_len + t:t_len + t + 1, :]
            for s in range(t):
                decay = jnp.exp(g_cum[t] - g_cum[s])
                seen = seen + (decay * gram[t:t + 1, s:s + 1]) * us[s]
                out = out + (decay * gram[t_len + t:t_len + t + 1, s:s + 1]) * us[s]
            u = beta[t:t + 1, :] * (v[t:t + 1, :] - seen)
            us.append(u)
            outs.append(out + gram[t_len + t:t_len + t + 1, t:t + 1] * u)
        g_end = g_cum[-1]
        kg = jnp.concatenate([k[t:t + 1, :] * jnp.exp(g_end - g_cum[t]) for t in range(t_len)] + [pad_rows], axis=0)
        u_all = jnp.concatenate(us + [pad_rows], axis=0)
        updates.append((jnp.exp(g_end), kg.astype(BF16), u_all.astype(BF16)))
        o = jnp.concatenate(outs, axis=0)
        o_ref[0, :, h * dh:(h + 1) * dh] = _rms_normalize(o, gain_ref[...]) * _silu(z_ref[0, :, h * dh:(h + 1) * dh])
    for h, (keep, kg, u_all) in enumerate(updates):
        s_ref[0, h] = keep * s0_ref[0, h] + _dot_tn(kg, u_all)


def _dn_sample(qkv, buf, z, ba, s0, p):
    bsz, t_len, _ = qkv.shape
    dh = DN_HEAD_DIM
    per_b = lambda *shape: pl.BlockSpec((1,) + shape, lambda b: (b,) + (0,) * len(shape))
    const = lambda shape: pl.BlockSpec(shape, lambda b: (0,) * len(shape), pipeline_mode=pl.Buffered(1))
    lanes_row = const((1, V7X_LANES))
    return pl.pallas_call(
        _dn_sample_kernel,
        grid=(bsz,),
        in_specs=[lanes_row, lanes_row, per_b(t_len, 3 * DN_WIDTH), per_b(CONV_WIDTH - 1, 3 * DN_WIDTH), per_b(t_len, DN_WIDTH),
                  per_b(t_len, V7X_LANES), per_b(DN_HEADS, dh, dh), const((CONV_WIDTH, 3 * DN_WIDTH)), const((1, dh))],
        out_specs=[per_b(t_len, DN_WIDTH), per_b(CONV_WIDTH - 1, 3 * DN_WIDTH), per_b(DN_HEADS, dh, dh)],
        out_shape=[jax.ShapeDtypeStruct((bsz, t_len, DN_WIDTH), F32),
                   jax.ShapeDtypeStruct((bsz, CONV_WIDTH - 1, 3 * DN_WIDTH), F32),
                   jax.ShapeDtypeStruct((bsz, DN_HEADS, dh, dh), F32)],
        compiler_params=_params("arbitrary"),
        name="dn_sample",
    )(p["dn_A_log"], p["dn_dt_bias"], qkv, buf, z, ba, s0, p["dn_conv_w"], p["dn_o_gain"])


def _block_diag(w):
    n, c, d = w.shape
    eye = jnp.eye(n, dtype=w.dtype)
    return (eye[:, None, :, None] * w[:, :, None, :]).reshape(n * c, n * d)


def _prepare_weights(d):
    p = {}
    for name in ("w_ffn1_in", "w_ffn1_out", "w_ffn2_in", "w_ffn2_out"):
        p[name] = d[name].astype(BF16)
    p["w_in_even"] = d["w_in_even"][0].astype(BF16)
    p["w_out_even"] = d["w_out_even"][0].astype(BF16)
    p["ones_bd"] = _block_diag(jnp.ones((SB_HEADS, SB_HEAD_DIM, SB_HEAD_DIM), BF16))
    p["q_gain"] = jnp.tile(d["sb_q_gain"][0], SB_HEADS).reshape(1, SB_WIDTH)
    p["k_gain"] = jnp.tile(d["sb_k_gain"][0], SB_HEADS).reshape(1, SB_WIDTH)
    p["lru_conv_w"] = d["lru_conv_w"][0]
    p["lru_conv_b"] = d["lru_conv_b"][0].reshape(1, LRU_WIDTH)
    p["lru_wa"] = _block_diag(d["lru_w_a"][0]).astype(BF16)
    p["lru_wi"] = _block_diag(d["lru_w_i"][0]).astype(BF16)
    p["lru_ba"] = d["lru_b_a"][0].reshape(1, LRU_WIDTH)
    p["lru_bi"] = d["lru_b_i"][0].reshape(1, LRU_WIDTH)
    p["lru_lambda"] = d["lru_lambda"][0].reshape(1, LRU_WIDTH)
    w_odd = d["w_in_odd"][0]
    p["w_qkvz"] = w_odd[:, :4 * DN_WIDTH].astype(BF16)
    p["w_ba"] = jnp.pad(w_odd[:, 4 * DN_WIDTH:], ((0, 0), (0, V7X_LANES - 2 * DN_HEADS))).astype(BF16)
    p["w_out_odd"] = d["w_out_odd"][0].astype(BF16)
    p["dn_conv_w"] = d["dn_conv_w"][0]
    decay_lanes = lambda a: jnp.pad(a, (DN_HEADS, V7X_LANES - 2 * DN_HEADS)).reshape(1, V7X_LANES)
    p["dn_A_log"] = decay_lanes(d["dn_A_log"][0])
    p["dn_dt_bias"] = decay_lanes(d["dn_dt_bias"][0])
    p["dn_o_gain"] = d["dn_o_gain"][0].reshape(1, DN_HEAD_DIM)
    idx = jnp.arange(DN_CHUNK)
    p["dn_low"] = (idx[None, :] <= idx[:, None]).astype(BF16)
    later = (idx[:, None] > idx[None, :]).astype(F32)
    p["dn_uo"] = jnp.concatenate([later, jnp.zeros((DN_CHUNK, DN_CHUNK), F32), jnp.ones((DN_CHUNK, 2 * DN_CHUNK), F32)], axis=1)
    return p


def _even_layer_mixer(x, group, d, p):
    kind, bsz, t_len = group
    q, k, v, xr, xg = _even_in(x, d["norm_mix"][0], p["w_in_even"], p["ones_bd"], p["q_gain"], p["k_gain"])
    seq = lambda a: a.reshape(bsz, t_len, a.shape[-1])
    if kind == "prompt":
        attn = _sb_prompt(seq(q), seq(k), seq(v), d["sb_bias"][0], _tri_suffix(ATT_TILE))
        rec, tail, h_last = _lru_prompt(seq(xr), seq(xg), p)
        h_last = h_last[:, 0]
    else:
        bias_rows = jnp.broadcast_to(jnp.tile(d["sb_bias"][0], t_len)[:, None], (t_len * SB_HEADS, PAGE_SIZE))
        cache_k = jnp.transpose(d["cache_k"][0], (0, 2, 3, 1))
        cache_v = jnp.transpose(d["cache_v"][0], (0, 2, 3, 1))
        attn = _sb_sample(seq(q), seq(k), seq(v), cache_k, cache_v, d["page_table"], bias_rows, _tri_suffix(PAGE_SIZE))
        tm = lambda a: jnp.swapaxes(a, 0, 1)
        rec, tail, h_last = _lru_sample(tm(seq(xr)), tm(seq(xg)), tm(d["state_lru_conv"][0]), d["state_lru_h"][0], p)
        rec, tail = tm(rec), tm(tail)
    x = _out_proj([attn.reshape(-1, SB_WIDTH), rec.reshape(-1, LRU_WIDTH)], p["w_out_even"], x)
    heads = lambda a: a.reshape(bsz, t_len, SB_HEADS, SB_HEAD_DIM)
    return x, (heads(k), heads(v), tail, h_last)


def _odd_layer_mixer(x, group, d, p):
    kind, bsz, t_len = group
    qkv, z, ba = _odd_in(x, d["norm_mix"][1], p["w_qkvz"], p["w_ba"])
    seq = lambda a: a.reshape(bsz, t_len, a.shape[-1])
    if kind == "prompt":
        o, state = _dn_prompt(seq(qkv), seq(z), seq(ba), p)
        tail = seq(qkv)[:, t_len - (CONV_WIDTH - 1):, :]
    else:
        o, tail, state = _dn_sample(seq(qkv), d["state_dn_conv"][0], seq(z), seq(ba), d["state_dn_S"][0], p)
    x = _out_proj([o.reshape(-1, DN_WIDTH)], p["w_out_odd"], x)
    return x, (tail, state)


def kernel(x_prompt, x_sample, cache_k, cache_v, state_lru_conv, state_lru_h, state_dn_conv, state_dn_S, page_table, norm_ffn1, w_ffn1_in, w_ffn1_out, norm_mix, norm_ffn2, w_ffn2_in, w_ffn2_out, w_in_even, sb_q_gain, sb_k_gain, sb_bias, lru_conv_w, lru_conv_b, lru_w_a, lru_b_a, lru_w_i, lru_b_i, lru_lambda, w_out_even, w_in_odd, dn_conv_w, dn_A_log, dn_dt_bias, dn_o_gain, w_out_odd):
    d = dict(locals())
    assert norm_ffn1.shape[0] == 2 and w_in_even.shape[0] == 1 and w_in_odd.shape[0] == 1, "two-layer trunk only"
    p = _prepare_weights(d)
    results = []
    for kind, x in (("prompt", x_prompt), ("sample", x_sample)):
        bsz, t_len, _ = x.shape
        group = (kind, bsz, t_len)
        x = x.reshape(bsz * t_len, D_MODEL)
        x = _ffn(x, norm_ffn1[0], p["w_ffn1_in"][0], p["w_ffn1_out"][0])
        x, even_state = _even_layer_mixer(x, group, d, p)
        x = _ffn(x, norm_ffn2[0], p["w_ffn2_in"][0], p["w_ffn2_out"][0])
        x = _ffn(x, norm_ffn1[1], p["w_ffn1_in"][1], p["w_ffn1_out"][1])
        x, odd_state = _odd_layer_mixer(x, group, d, p)
        x = _ffn(x, norm_ffn2[1], p["w_ffn2_in"][1], p["w_ffn2_out"][1])
        results.append((x.reshape(bsz, t_len, D_MODEL),) + tuple(s[None] for s in even_state + odd_state))
    (yp, *prompt_state), (ys, *sample_state) = results
    return (yp, ys, *prompt_state, *sample_state)
```

```python
import functools
import math

import jax
import jax.numpy as jnp
from jax import lax
from jax.experimental import pallas as pl
from jax.experimental.pallas import tpu as pltpu

F32 = jnp.float32
BF16 = jnp.bfloat16

V7X_LANES = 128
V7X_SUBLANES = 8
V7X_VMEM_LIMIT_BYTES = 56 * 1024 * 1024

D_MODEL = 1024
D_FF = 2 * D_MODEL
SB_HEADS = 8
SB_HEAD_DIM = 64
SB_WIDTH = SB_HEADS * SB_HEAD_DIM
LRU_WIDTH = 512
LRU_BLOCKS = 8
LRU_C = 8.0
CONV_WIDTH = 4
DN_HEADS = 8
DN_HEAD_DIM = 128
DN_WIDTH = DN_HEADS * DN_HEAD_DIM
DN_CHUNK = 64
PAGE_SIZE = 128
EPS = 1e-6

TOKEN_TILE = 512
FF_CHUNK = 512
ATT_TILE = 256
LRU_TIME_TILE = 512


def _params(*sem):
    return pltpu.CompilerParams(dimension_semantics=sem, vmem_limit_bytes=V7X_VMEM_LIMIT_BYTES)


def _resident(shape):
    nd = len(shape)
    return pl.BlockSpec(shape, lambda *_: (0,) * nd, pipeline_mode=pl.Buffered(1))


def _rms_normalize(x, gain):
    ms = jnp.mean(x * x, axis=-1, keepdims=True)
    return x * lax.rsqrt(ms + EPS) * gain


LOG2E = 1.0 / math.log(2.0)


def _softplus(x):
    return jnp.maximum(x, 0.0) + jnp.log(1.0 + jnp.exp2(jnp.abs(x) * (-LOG2E)))


def _split2(x):
    hi = x.astype(BF16)
    lo = (x - hi.astype(F32)).astype(BF16)
    return hi, lo


def _split3(x):
    hi = x.astype(BF16)
    r1 = x - hi.astype(F32)
    mid = r1.astype(BF16)
    lo = (r1 - mid.astype(F32)).astype(BF16)
    return hi, mid, lo


def _dot(a, b):
    return jnp.dot(a, b, preferred_element_type=F32)


def _dot_nt(a, b):
    return lax.dot_general(a, b, (((1,), (1,)), ((), ())), preferred_element_type=F32)


def _dot_tn(a, b):
    return lax.dot_general(a, b, (((0,), (0,)), ((), ())), preferred_element_type=F32)


def _ffn_half_step(x, g_ref, win_ref, wout_ref, act_ref):
    xn = _rms_normalize(x, g_ref[...]).astype(BF16)
    for c in range(D_FF // FF_CHUNK):
        lo = c * FF_CHUNK
        gate = _dot(xn, win_ref[:, lo:lo + FF_CHUNK])
        up = _dot(xn, win_ref[:, D_FF + lo:D_FF + lo + FF_CHUNK])
        act_ref[:, lo:lo + FF_CHUNK] = (gate * jax.nn.sigmoid(gate) * up).astype(BF16)
    return x + 0.5 * _dot(act_ref[...], wout_ref[...])


def _ffn_kernel(x_ref, g_ref, win_ref, wout_ref, o_ref, act_ref):
    o_ref[...] = _ffn_half_step(x_ref[...], g_ref, win_ref, wout_ref, act_ref)


def _mix_ffn_kernel(*refs, n_parts):
    parts, (wmix_ref, x_ref, g_ref, win_ref, wout_ref, o_ref, act_ref) = refs[:n_parts], refs[n_parts:]
    x = x_ref[...]
    row = 0
    for p in parts:
        width = p.shape[-1]
        x = x + _dot(p[...].astype(BF16), wmix_ref[row:row + width, :])
        row += width
    o_ref[...] = _ffn_half_step(x, g_ref, win_ref, wout_ref, act_ref)


def _mix_ffn(parts, w_mix, x, gain, w_in, w_out):
    n = x.shape[0]
    tm = min(TOKEN_TILE, n)
    row = lambda i: (i, 0)
    return pl.pallas_call(
        functools.partial(_mix_ffn_kernel, n_parts=len(parts)),
        grid=(n // tm,),
        in_specs=[pl.BlockSpec((tm, p.shape[1]), row) for p in parts]
        + [_resident(w_mix.shape), pl.BlockSpec((tm, D_MODEL), row), _resident((1, D_MODEL)),
           _resident((D_MODEL, 2 * D_FF)), _resident((D_FF, D_MODEL))],
        out_specs=pl.BlockSpec((tm, D_MODEL), row),
        out_shape=jax.ShapeDtypeStruct((n, D_MODEL), F32),
        scratch_shapes=[pltpu.VMEM((tm, D_FF), BF16)],
        compiler_params=_params("arbitrary"),
        name="mix_ffn",
    )(*parts, w_mix, x, gain.reshape(1, D_MODEL), w_in, w_out)


def _ffn(x, gain, w_in, w_out):
    n = x.shape[0]
    tm = min(TOKEN_TILE, n)
    return pl.pallas_call(
        _ffn_kernel,
        grid=(n // tm,),
        in_specs=[
            pl.BlockSpec((tm, D_MODEL), lambda i: (i, 0)),
            _resident((1, D_MODEL)),
            _resident((D_MODEL, 2 * D_FF)),
            _resident((D_FF, D_MODEL)),
        ],
        out_specs=pl.BlockSpec((tm, D_MODEL), lambda i: (i, 0)),
        out_shape=jax.ShapeDtypeStruct((n, D_MODEL), F32),
        scratch_shapes=[pltpu.VMEM((tm, D_FF), BF16)],
        compiler_params=_params("arbitrary"),
        name="ffn",
    )(x, gain.reshape(1, D_MODEL), w_in, w_out)


def _head_rms(x, ones_bd, gain):
    hi, lo = _split2(x * x)
    ms = (_dot(hi, ones_bd) + _dot(lo, ones_bd)) * (1.0 / SB_HEAD_DIM)
    return x * lax.rsqrt(ms + EPS) * gain


def _even_in_kernel(x_ref, g_ref, w_ref, ones_ref, qg_ref, kg_ref,
                    q_ref, k_ref, v_ref, xr_ref, xg_ref):
    xn = _rms_normalize(x_ref[...], g_ref[...]).astype(BF16)
    ones_bd = ones_ref[...]
    q = _dot(xn, w_ref[:, 0:SB_WIDTH])
    q_ref[...] = _head_rms(q, ones_bd, qg_ref[...])
    k = _dot(xn, w_ref[:, SB_WIDTH:2 * SB_WIDTH])
    k_ref[...] = _head_rms(k, ones_bd, kg_ref[...])
    v_ref[...] = _dot(xn, w_ref[:, 2 * SB_WIDTH:3 * SB_WIDTH])
    xr_ref[...] = _dot(xn, w_ref[:, 3 * SB_WIDTH:3 * SB_WIDTH + LRU_WIDTH])
    xg_ref[...] = _dot(xn, w_ref[:, 3 * SB_WIDTH + LRU_WIDTH:])


def _even_in(x, gain, w, ones_bd, q_gain, k_gain):
    n = x.shape[0]
    tm = min(TOKEN_TILE, n)
    width = w.shape[1]
    row = lambda i: (i, 0)
    out = jax.ShapeDtypeStruct((n, SB_WIDTH), F32)
    return pl.pallas_call(
        _even_in_kernel,
        grid=(n // tm,),
        in_specs=[
            pl.BlockSpec((tm, D_MODEL), row),
            _resident((1, D_MODEL)),
            _resident((D_MODEL, width)),
            _resident((SB_WIDTH, SB_WIDTH)),
            _resident((1, SB_WIDTH)),
            _resident((1, SB_WIDTH)),
        ],
        out_specs=[pl.BlockSpec((tm, SB_WIDTH), row)] * 5,
        out_shape=[out] * 5,
        compiler_params=_params("arbitrary"),
        name="even_in",
    )(x, gain.reshape(1, D_MODEL), w, ones_bd, q_gain, k_gain)


def _gelu_tanh(x):
    cdf = 0.5 * (1.0 + jnp.tanh(math.sqrt(2.0 / math.pi) * (x + 0.044715 * (x * x * x))))
    return x * cdf


def _lru_coeffs(xc, wa_ref, ba_ref, wi_ref, bi_ref, sp_lam):
    xb = xc.astype(BF16)
    r = jax.nn.sigmoid(_dot(xb, wa_ref[...]) + ba_ref[...])
    i = jax.nn.sigmoid(_dot(xb, wi_ref[...]) + bi_ref[...])
    log_a = -LRU_C * r * sp_lam
    a = jnp.exp(log_a)
    b = jnp.sqrt(-jnp.tanh(log_a) * (a * a + 1.0)) * (i * xc)
    return a, b


def _lru_prompt_kernel(xr_ref, xg_ref, cw_ref, cb_ref, wa_ref, ba_ref, wi_ref, bi_ref, lam_ref,
                       rec_ref, tail_ref, hlast_ref, xp_ref, a_ref, b_ref, h_ref, hc_ref):
    t = pl.program_id(1)
    tt = xr_ref.shape[1]
    pad = V7X_SUBLANES

    @pl.when(t == 0)
    def _():
        xp_ref[0:pad, :] = jnp.zeros((pad, LRU_WIDTH), F32)
        hc_ref[...] = jnp.zeros_like(hc_ref)

    @pl.when(t > 0)
    def _():
        xp_ref[0:pad, :] = xp_ref[tt:tt + pad, :]

    xp_ref[pad:pad + tt, :] = xr_ref[0]
    xc = cw_ref[0:1, :] * xp_ref[pad - 3:pad - 3 + tt, :]
    for j in range(1, CONV_WIDTH):
        xc = xc + cw_ref[j:j + 1, :] * xp_ref[pad - 3 + j:pad - 3 + j + tt, :]
    xc = xc + cb_ref[...]
    a, b = _lru_coeffs(xc, wa_ref, ba_ref, wi_ref, bi_ref, _softplus(-lam_ref[...]))
    groups = tt // V7X_SUBLANES
    ag = a.reshape(groups, V7X_SUBLANES, LRU_WIDTH)
    bg = b.reshape(groups, V7X_SUBLANES, LRU_WIDTH)
    row = lax.broadcasted_iota(jnp.int32, ag.shape, 1)
    for d in (1, 2, 4):
        keep = row >= d
        a_prev = jnp.where(keep, pltpu.roll(ag, d, axis=1), 1.0)
        b_prev = jnp.where(keep, pltpu.roll(bg, d, axis=1), 0.0)
        bg = ag * b_prev + bg
        ag = ag * a_prev
    a_ref[...] = ag[:, V7X_SUBLANES - 1, :]
    b_ref[...] = bg[:, V7X_SUBLANES - 1, :]

    def enter(j, h):
        h_ref[pl.ds(j, 1), :] = h
        return a_ref[pl.ds(j, 1), :] * h + b_ref[pl.ds(j, 1), :]

    h_end = lax.fori_loop(0, groups, enter, hc_ref[...])
    hc_ref[...] = h_end
    h_all = (ag * h_ref[...][:, None, :] + bg).reshape(tt, LRU_WIDTH)
    rec_ref[0] = h_all * _gelu_tanh(xg_ref[0])

    @pl.when(t == pl.num_programs(1) - 1)
    def _():
        tail_ref[0] = xp_ref[pad + tt - 3:pad + tt, :]
        hlast_ref[0] = h_end


def _lru_prompt(xr, xg, p):
    bsz, t_len, w = xr.shape
    tt = min(LRU_TIME_TILE, t_len)
    tile = pl.BlockSpec((1, tt, w), lambda b, t: (b, t, 0))
    per_b = lambda rows: pl.BlockSpec((1, rows, w), lambda b, t: (b, 0, 0))
    return pl.pallas_call(
        _lru_prompt_kernel,
        grid=(bsz, t_len // tt),
        in_specs=[tile, tile, _resident((CONV_WIDTH, w)), _resident((1, w)), _resident((w, w)),
                  _resident((1, w)), _resident((w, w)), _resident((1, w)), _resident((1, w))],
        out_specs=[tile, per_b(CONV_WIDTH - 1), per_b(1)],
        out_shape=[jax.ShapeDtypeStruct((bsz, t_len, w), F32),
                   jax.ShapeDtypeStruct((bsz, CONV_WIDTH - 1, w), F32),
                   jax.ShapeDtypeStruct((bsz, 1, w), F32)],
        scratch_shapes=[pltpu.VMEM((tt + 2 * V7X_SUBLANES, w), F32)]
        + [pltpu.VMEM((tt // V7X_SUBLANES, w), F32)] * 3 + [pltpu.VMEM((1, w), F32)],
        compiler_params=_params("arbitrary", "arbitrary"),
        name="lru_prompt",
    )(xr, xg, p["lru_conv_w"], p["lru_conv_b"], p["lru_wa"], p["lru_ba"], p["lru_wi"], p["lru_bi"],
      p["lru_lambda"])


def _lru_sample_kernel(xr_ref, xg_ref, buf_ref, h0_ref, cw_ref, cb_ref, wa_ref, ba_ref, wi_ref, bi_ref,
                       lam_ref, rec_ref, tail_ref, hlast_ref):
    t_len = xr_ref.shape[0]
    xp = [buf_ref[j] for j in range(CONV_WIDTH - 1)] + [xr_ref[j] for j in range(t_len)]
    sp_lam = _softplus(-lam_ref[...])
    h = h0_ref[...]
    for t in range(t_len):
        xc = cw_ref[0:1, :] * xp[t]
        for j in range(1, CONV_WIDTH):
            xc = xc + cw_ref[j:j + 1, :] * xp[t + j]
        xc = xc + cb_ref[...]
        a, b = _lru_coeffs(xc, wa_ref, ba_ref, wi_ref, bi_ref, sp_lam)
        h = a * h + b
        rec_ref[t] = h * _gelu_tanh(xg_ref[t])
    for j in range(CONV_WIDTH - 1):
        tail_ref[j] = xp[t_len + j]
    hlast_ref[...] = h


def _lru_sample(xr, xg, buf, h0, p):
    t_len, bsz, w = xr.shape
    full = lambda shape: pl.BlockSpec(shape, lambda i: (0,) * len(shape))
    return pl.pallas_call(
        _lru_sample_kernel,
        grid=(1,),
        in_specs=[full((t_len, bsz, w)), full((t_len, bsz, w)), full((CONV_WIDTH - 1, bsz, w)), full((bsz, w)),
                  full((CONV_WIDTH, w)), full((1, w)), full((w, w)), full((1, w)), full((w, w)), full((1, w)),
                  full((1, w))],
        out_specs=[full((t_len, bsz, w)), full((CONV_WIDTH - 1, bsz, w)), full((bsz, w))],
        out_shape=[jax.ShapeDtypeStruct((t_len, bsz, w), F32),
                   jax.ShapeDtypeStruct((CONV_WIDTH - 1, bsz, w), F32),
                   jax.ShapeDtypeStruct((bsz, w), F32)],
        compiler_params=_params("arbitrary"),
        name="lru_sample",
    )(xr, xg, buf, h0, p["lru_conv_w"], p["lru_conv_b"], p["lru_wa"], p["lru_ba"], p["lru_wi"], p["lru_bi"],
      p["lru_lambda"])


def _sb_chains(scores, masks, bias, tri, carry, one_dot):
    rows = scores[0].shape[0]
    zs, parts = [], []
    for s, m in zip(scores, masks):
        z = s + bias
        sp = _softplus(z)
        if m is not None:
            sp = jnp.where(m, sp, 0.0)
        zs.append(z)
        parts.append(jnp.concatenate(_split2(sp), axis=1))
    if one_dot:
        cum = _dot(jnp.concatenate(parts, axis=0), tri)
        cums = [cum[i * rows:(i + 1) * rows] for i in range(len(parts))]
    else:
        cums = [_dot(part, tri) for part in parts]
    ws = []
    for z, c, m in zip(zs, cums, masks):
        w = jnp.exp(z - c - carry)
        if m is not None:
            w = jnp.where(m, w, 0.0)
        ws.append(w.astype(BF16))
        carry = carry + c[:, 0:1]
    return carry, ws


def _sb_prompt_kernel(bias_ref, q_ref, k_ref, v_ref, tri_ref, o_ref):
    hp, qi = pl.program_id(1), pl.program_id(2)
    tq = q_ref.shape[1]
    kw = tq // 2
    q = q_ref[0] * (SB_HEAD_DIM ** -0.5)
    lane = lax.broadcasted_iota(jnp.int32, (tq, 2 * SB_HEAD_DIM), 1)
    q2 = jnp.concatenate([jnp.where(lane < SB_HEAD_DIM, q, 0.0), jnp.where(lane < SB_HEAD_DIM, 0.0, q)], axis=0)
    q2 = q2.astype(BF16)
    first_head = lax.broadcasted_iota(jnp.int32, (2 * tq, 1), 0) < tq
    bias = jnp.where(first_head, bias_ref[hp * 2], bias_ref[hp * 2 + 1])
    tri = tri_ref[...]

    def pair_step(j, masks, carry, acc):
        start = pl.multiple_of(j * tq, tq)
        newer, older = pl.ds(start + kw, kw), pl.ds(start, kw)
        scores = [_dot_nt(q2, k_ref[0, newer, :].astype(BF16)), _dot_nt(q2, k_ref[0, older, :].astype(BF16))]
        carry, ws = _sb_chains(scores, masks, bias, tri, carry, one_dot=False)
        acc = acc + _dot(ws[0], v_ref[0, newer, :].astype(BF16)) + _dot(ws[1], v_ref[0, older, :].astype(BF16))
        return carry, acc

    q_pos = lax.broadcasted_iota(jnp.int32, (2 * tq, kw), 0) & (tq - 1)
    k_pos = lax.broadcasted_iota(jnp.int32, (2 * tq, kw), 1)
    state = (jnp.zeros((2 * tq, 1), F32), jnp.zeros((2 * tq, 2 * SB_HEAD_DIM), F32))
    state = pair_step(qi, [k_pos + kw < q_pos, k_pos < q_pos], *state)
    state = lax.fori_loop(0, qi, lambda it, st: pair_step(qi - 1 - it, [None, None], *st), state)
    acc = state[1]
    o_ref[0] = jnp.where(lane < SB_HEAD_DIM, acc[:tq], acc[tq:])


def _sb_prompt(q, k, v, bias, tri):
    bsz, t_len, width = q.shape
    tq = 2 * ATT_TILE
    assert t_len % tq == 0 and tq & (tq - 1) == 0
    pair = 2 * SB_HEAD_DIM
    tile = pl.BlockSpec((1, tq, pair), lambda b, hp, qi: (b, qi, hp))
    seq = pl.BlockSpec((1, t_len, pair), lambda b, hp, qi: (b, 0, hp))
    return pl.pallas_call(
        _sb_prompt_kernel,
        grid=(bsz, width // pair, t_len // tq),
        in_specs=[pl.BlockSpec(memory_space=pltpu.SMEM), tile, seq, seq,
                  pl.BlockSpec((tq, ATT_TILE), lambda b, hp, qi: (0, 0), pipeline_mode=pl.Buffered(1))],
        out_specs=tile,
        out_shape=jax.ShapeDtypeStruct((bsz, t_len, width), F32),
        compiler_params=_params("arbitrary", "arbitrary", "arbitrary"),
        name="sb_prompt",
    )(bias, q, k, v, tri)


def _sb_sample_kernel(pt_ref, q_ref, kn_ref, vn_ref, bias_ref, tri_ref, *refs, n_pages):
    del pt_ref
    k_pages, v_pages, o_ref = refs[:n_pages], refs[n_pages:2 * n_pages], refs[2 * n_pages]
    t_len, width = q_ref.shape[1], q_ref.shape[2]
    rows = t_len * SB_HEADS
    q = q_ref[0] * (SB_HEAD_DIM ** -0.5)
    qb = jnp.broadcast_to(q[:, None, :], (t_len, SB_HEADS, width)).reshape(rows, width)
    row = lax.broadcasted_iota(jnp.int32, (rows, width), 0)
    lane = lax.broadcasted_iota(jnp.int32, (rows, width), 1)
    own_head = (row & (SB_HEADS - 1)) == (lane >> int(math.log2(SB_HEAD_DIM)))
    qbd = jnp.where(own_head, qb, 0.0).astype(BF16)
    page = lambda ref: ref[...].reshape(width, PAGE_SIZE).astype(BF16)

    pad = jnp.zeros((PAGE_SIZE - t_len, width), F32)
    k_new = jnp.concatenate([kn_ref[0], pad], axis=0).astype(BF16)
    v_new = jnp.concatenate([vn_ref[0], pad], axis=0).astype(BF16)
    key = lax.broadcasted_iota(jnp.int32, (rows, PAGE_SIZE), 1)
    tok = lax.broadcasted_iota(jnp.int32, (rows, PAGE_SIZE), 0) >> int(math.log2(SB_HEADS))
    order = list(reversed(range(n_pages)))
    scores = [_dot_nt(qbd, k_new)] + [_dot(qbd, page(k_pages[p])) for p in order]
    masks = [key < tok] + [None] * n_pages
    _, ws = _sb_chains(scores, masks, bias_ref[...], tri_ref[...], jnp.zeros((rows, 1), F32), one_dot=True)
    acc = _dot(ws[0], v_new)
    for w, p in zip(ws[1:], order):
        acc = acc + _dot_nt(w, page(v_pages[p]))
    acc = jnp.where(own_head, acc, 0.0)
    o_ref[0] = jnp.sum(acc.reshape(t_len, SB_HEADS, width), axis=1)


def _sb_sample(q, k_new, v_new, cache_k, cache_v, page_table, bias_rows, tri):
    bsz, t_len, width = q.shape
    n_pages = page_table.shape[1]
    rows = t_len * SB_HEADS
    tok = pl.BlockSpec((1, t_len, width), lambda b, pt: (b, 0, 0))
    const = lambda shape: pl.BlockSpec(shape, lambda b, pt: (0, 0), pipeline_mode=pl.Buffered(1))
    pages = [pl.BlockSpec((None, SB_HEADS, SB_HEAD_DIM, PAGE_SIZE), lambda b, pt, j=j: (pt[b, j], 0, 0, 0))
             for j in range(n_pages)]
    grid_spec = pltpu.PrefetchScalarGridSpec(
        num_scalar_prefetch=1,
        grid=(bsz,),
        in_specs=[tok, tok, tok, const((rows, PAGE_SIZE)), const((2 * PAGE_SIZE, PAGE_SIZE))] + pages + pages,
        out_specs=tok,
    )
    return pl.pallas_call(
        functools.partial(_sb_sample_kernel, n_pages=n_pages),
        grid_spec=grid_spec,
        out_shape=jax.ShapeDtypeStruct((bsz, t_len, width), F32),
        compiler_params=_params("arbitrary"),
        name="sb_sample",
    )(page_table, q, k_new, v_new, bias_rows, tri, *([cache_k] * n_pages), *([cache_v] * n_pages))


def _tri_suffix(n):
    idx = jnp.arange(n)
    tri = (idx[:, None] >= idx[None, :]).astype(BF16)
    return jnp.concatenate([tri, tri], axis=0)


ODD_COL_CHUNK = 1024
DN_TIME_TILE = 1024
DN_GROUP = 2


def _odd_in_kernel(x_ref, g_ref, w_ref, wba_ref, qkv_ref, z_ref, ba_ref):
    xn = _rms_normalize(x_ref[...], g_ref[...]).astype(BF16)
    for c in range(3 * DN_WIDTH // ODD_COL_CHUNK):
        lo = c * ODD_COL_CHUNK
        qkv_ref[:, lo:lo + ODD_COL_CHUNK] = _dot(xn, w_ref[:, lo:lo + ODD_COL_CHUNK])
    z_ref[...] = _dot(xn, w_ref[:, 3 * DN_WIDTH:])
    ba_ref[...] = _dot(xn, wba_ref[...])


def _odd_in(x, gain, w_qkvz, w_ba):
    n = x.shape[0]
    tm = min(TOKEN_TILE, n)
    row = lambda i: (i, 0)
    return pl.pallas_call(
        _odd_in_kernel,
        grid=(n // tm,),
        in_specs=[pl.BlockSpec((tm, D_MODEL), row), _resident((1, D_MODEL)),
                  _resident(w_qkvz.shape), _resident(w_ba.shape)],
        out_specs=[pl.BlockSpec((tm, 3 * DN_WIDTH), row), pl.BlockSpec((tm, DN_WIDTH), row),
                   pl.BlockSpec((tm, V7X_LANES), row)],
        out_shape=[jax.ShapeDtypeStruct((n, 3 * DN_WIDTH), F32), jax.ShapeDtypeStruct((n, DN_WIDTH), F32),
                   jax.ShapeDtypeStruct((n, V7X_LANES), F32)],
        compiler_params=_params("arbitrary"),
        name="odd_in",
    )(x, gain.reshape(1, D_MODEL), w_qkvz, w_ba)


def _l2_normalize(x):
    return x * lax.rsqrt(jnp.sum(x * x, axis=-1, keepdims=True) + EPS)


def _silu(x):
    return x * jax.nn.sigmoid(x)


def _head_column(x, lane_index):
    lane = lax.broadcasted_iota(jnp.int32, x.shape, 1)
    return jnp.sum(jnp.where(lane == lane_index, x, 0.0), axis=-1, keepdims=True)


def _dn_gates(ba, alog_row, dtb_row):
    beta = jax.nn.sigmoid(ba)
    g = -jnp.exp(alog_row) * _softplus(ba + dtb_row)
    return beta, g


def _dn_prompt_kernel(alog_ref, dtb_ref, q_ref, k_ref, v_ref, z_ref, ba_ref, cwq_ref, cwk_ref, cwv_ref,
                      gain_ref, low_ref, uo_ref, o_ref, s_ref, xq_ref, xk_ref, xv_ref, state_ref):
    grp, t = pl.program_id(1), pl.program_id(2)
    tt, gw = q_ref.shape[1], q_ref.shape[2]
    dh, cs, pad = DN_HEAD_DIM, DN_CHUNK, V7X_SUBLANES
    n_heads, n_chunks = gw // dh, tt // cs

    @pl.when(t == 0)
    def _():
        state_ref[...] = jnp.zeros_like(state_ref)

    def conv_silu(x_ref, xp_ref, cw_ref):
        @pl.when(t == 0)
        def _():
            xp_ref[0:pad, :] = jnp.zeros((pad, gw), F32)

        @pl.when(t > 0)
        def _():
            xp_ref[0:pad, :] = xp_ref[tt:tt + pad, :]

        xp_ref[pad:pad + tt, :] = x_ref[0]
        y = cw_ref[0:1, :] * xp_ref[pad - 3:pad - 3 + tt, :]
        for j in range(1, CONV_WIDTH):
            y = y + cw_ref[j:j + 1, :] * xp_ref[pad - 3 + j:pad - 3 + j + tt, :]
        return _silu(y)

    q_c, k_c, v_c = conv_silu(q_ref, xq_ref, cwq_ref), conv_silu(k_ref, xk_ref, cwk_ref), conv_silu(v_ref, xv_ref, cwv_ref)
    beta_all, g_all = _dn_gates(ba_ref[0], alog_ref[...], dtb_ref[...])
    per_head = {name: [] for name in ("q", "k", "v", "beta", "g")}
    for h in range(n_heads):
        lanes = slice(h * dh, (h + 1) * dh)
        head = grp * n_heads + h
        per_head["q"].append(_l2_normalize(q_c[:, lanes]) * (dh ** -0.5))
        per_head["k"].append(_l2_normalize(k_c[:, lanes]))
        per_head["v"].append(v_c[:, lanes])
        per_head["beta"].append(_head_column(beta_all, head))
        per_head["g"].append(_head_column(g_all, DN_HEADS + head))
    nb = n_heads * n_chunks
    stack = lambda name: jnp.concatenate(per_head[name], axis=0).reshape(nb, cs, -1)
    qc, kc, vc, beta, g = stack("q"), stack("k"), stack("v"), stack("beta"), stack("g")

    bmm = lambda a, b: lax.dot_general(a, b, (((2,), (1,)), ((0,), (0,))), preferred_element_type=F32)
    bmm_nt = lambda a, b: lax.dot_general(a, b, (((2,), (2,)), ((0,), (0,))), preferred_element_type=F32)
    low = jnp.broadcast_to(low_ref[...][None], (nb, cs, cs))
    row = lax.broadcasted_iota(jnp.int32, (nb, cs, cs), 1)
    col = lax.broadcasted_iota(jnp.int32, (nb, cs, cs), 2)
    parts = _split3(g * uo_ref[...][None])
    gfull = bmm(low, parts[0]) + bmm(low, parts[1]) + bmm(low, parts[2])
    decay = jnp.exp(gfull[:, :, 0:cs])
    g_col = gfull[:, :, 2 * cs:]
    e_g = jnp.exp(g_col)
    g_last = g_col[:, cs - 1:cs, :]
    kb = kc.astype(BF16)
    a_mat = jnp.where(col < row, bmm_nt(kb, kb) * decay, 0.0) * beta
    p_mat = jnp.where(col <= row, bmm_nt(qc.astype(BF16), kb) * decay, 0.0).astype(BF16)
    rhs = jnp.concatenate([vc * beta, kc * (beta * e_g)], axis=2)
    t_off, x = -a_mat, a_mat
    for _ in range(int(math.log2(cs)) - 1):
        xb = x.astype(BF16)
        x = bmm(xb, xb)
        t_off = t_off + x + bmm(t_off.astype(BF16), x.astype(BF16))
    sol = rhs + bmm(t_off.astype(BF16), rhs.astype(BF16))
    uv, wk = sol[:, :, 0:dh], sol[:, :, dh:].astype(BF16)
    qg = (qc * e_g).astype(BF16)
    kg = (kc * jnp.exp(g_last - g_col)).astype(BF16)
    g_end = jnp.exp(g_last)

    bmm_tn = lambda a, b: lax.dot_general(a, b, (((1,), (1,)), ((0,), (0,))), preferred_element_type=F32)
    mix = bmm_tn(kg, wk).astype(BF16)
    fresh = bmm_tn(kg, uv.astype(BF16))
    states = [state_ref[h] for h in range(n_heads)]
    entering = [[None] * n_chunks for _ in range(n_heads)]
    for c in range(n_chunks):
        for h in range(n_heads):
            i = h * n_chunks + c
            sb = states[h].astype(BF16)
            entering[h][c] = sb
            states[h] = g_end[i] * states[h] - _dot(mix[i], sb) + fresh[i]
    for h in range(n_heads):
        state_ref[h] = states[h]
    s_in = jnp.stack([entering[h][c] for h in range(n_heads) for c in range(n_chunks)], axis=0)
    ws = bmm(jnp.concatenate([wk, qg], axis=1), s_in)
    ub = (uv - ws[:, 0:cs, :]).astype(BF16)
    o = ws[:, cs:, :] + bmm(p_mat, ub)
    for h in range(n_heads):
        lanes = slice(h * dh, (h + 1) * dh)
        o_h = o[h * n_chunks:(h + 1) * n_chunks].reshape(tt, dh)
        o_ref[0, :, lanes] = _rms_normalize(o_h, gain_ref[...]) * _silu(z_ref[0, :, lanes])

    @pl.when(t == pl.num_programs(2) - 1)
    def _():
        for h in range(n_heads):
            s_ref[0, h] = states[h]


def _dn_prompt(qkv, z, ba, p):
    bsz, t_len, _ = qkv.shape
    tt = min(DN_TIME_TILE, t_len)
    dh = DN_HEAD_DIM
    gw = DN_GROUP * dh
    groups = DN_HEADS // DN_GROUP
    lane_blk = lambda off: pl.BlockSpec((1, tt, gw), lambda b, g, t: (b, t, off + g))
    cw_blk = lambda off: pl.BlockSpec((CONV_WIDTH, gw), lambda b, g, t: (0, off + g))
    const = lambda shape: pl.BlockSpec(shape, lambda b, g, t: (0,) * len(shape), pipeline_mode=pl.Buffered(1))
    lanes_row = const((1, V7X_LANES))
    return pl.pallas_call(
        _dn_prompt_kernel,
        grid=(bsz, groups, t_len // tt),
        in_specs=[lanes_row, lanes_row, lane_blk(0), lane_blk(groups), lane_blk(2 * groups), lane_blk(0),
                  pl.BlockSpec((1, tt, V7X_LANES), lambda b, g, t: (b, t, 0)),
                  cw_blk(0), cw_blk(groups), cw_blk(2 * groups),
                  const((1, dh)), const((DN_CHUNK, DN_CHUNK)), const((DN_CHUNK, 4 * DN_CHUNK))],
        out_specs=[lane_blk(0), pl.BlockSpec((1, DN_GROUP, dh, dh), lambda b, g, t: (b, g, 0, 0))],
        out_shape=[jax.ShapeDtypeStruct((bsz, t_len, DN_WIDTH), F32),
                   jax.ShapeDtypeStruct((bsz, DN_HEADS, dh, dh), F32)],
        scratch_shapes=[pltpu.VMEM((tt + 2 * V7X_SUBLANES, gw), F32)] * 3 + [pltpu.VMEM((DN_GROUP, dh, dh), F32)],
        compiler_params=_params("arbitrary", "arbitrary", "arbitrary"),
        name="dn_prompt",
    )(p["dn_A_log"], p["dn_dt_bias"], qkv, qkv, qkv, z, ba, p["dn_conv_w"], p["dn_conv_w"], p["dn_conv_w"],
      p["dn_o_gain"], p["dn_low"], p["dn_uo"])


def _dn_sample_kernel(alog_ref, dtb_ref, qkv_ref, buf_ref, z_ref, ba_ref, s0_ref, cw_ref, gain_ref,
                      o_ref, tail_ref, s_ref):
    t_len = qkv_ref.shape[1]
    dh = DN_HEAD_DIM
    xp = jnp.concatenate([buf_ref[0], qkv_ref[0]], axis=0)
    y = cw_ref[0:1, :] * xp[0:t_len]
    for j in range(1, CONV_WIDTH):
        y = y + cw_ref[j:j + 1, :] * xp[j:j + t_len]
    y = _silu(y)
    tail_ref[0] = xp[t_len:t_len + CONV_WIDTH - 1]
    beta_all, g_all = _dn_gates(ba_ref[0], alog_ref[...], dtb_ref[...])
    pad_rows = jnp.zeros((V7X_SUBLANES - t_len, dh), F32)
    ks, bases, grams = [], [], []
    for h in range(DN_HEADS):
        q = _l2_normalize(y[:, h * dh:(h + 1) * dh]) * (dh ** -0.5)
        k = _l2_normalize(y[:, DN_WIDTH + h * dh:DN_WIDTH + (h + 1) * dh])
        kq = jnp.concatenate([k, q], axis=0).astype(BF16)
        ks.append(k)
        bases.append(_dot(kq, s0_ref[0, h].astype(BF16)))
        grams.append(_dot_nt(kq, kq))
    updates = []
    for h in range(DN_HEADS):
        k, base, gram = ks[h], bases[h], grams[h]
        v = y[:, 2 * DN_WIDTH + h * dh:2 * DN_WIDTH + (h + 1) * dh]
        beta, g = _head_column(beta_all, h), _head_column(g_all, DN_HEADS + h)
        g_cum = [g[0:1, :]]
        for t in range(1, t_len):
            g_cum.append(g_cum[-1] + g[t:t + 1, :])
        us, outs = [], []
        for t in range(t_len):
            seen = jnp.exp(g_cum[t]) * base[t:t + 1, :]
            out = jnp.exp(g_cum[t]) * base[t_len + t:t_len + t + 1, :]
            for s in range(t):
                decay = jnp.exp(g_cum[t] - g_cum[s])
                seen = seen + (decay * gram[t:t + 1, s:s + 1]) * us[s]
                out = out + (decay * gram[t_len + t:t_len + t + 1, s:s + 1]) * us[s]
            u = beta[t:t + 1, :] * (v[t:t + 1, :] - seen)
            us.append(u)
            outs.append(out + gram[t_len + t:t_len + t + 1, t:t + 1] * u)
        g_end = g_cum[-1]
        kg = jnp.concatenate([k[t:t + 1, :] * jnp.exp(g_end - g_cum[t]) for t in range(t_len)] + [pad_rows], axis=0)
        u_all = jnp.concatenate(us + [pad_rows], axis=0)
        updates.append((jnp.exp(g_end), kg.astype(BF16), u_all.astype(BF16)))
        o = jnp.concatenate(outs, axis=0)
        o_ref[0, :, h * dh:(h + 1) * dh] = _rms_normalize(o, gain_ref[...]) * _silu(z_ref[0, :, h * dh:(h + 1) * dh])
    for h, (keep, kg, u_all) in enumerate(updates):
        s_ref[0, h] = keep * s0_ref[0, h] + _dot_tn(kg, u_all)


def _dn_sample(qkv, buf, z, ba, s0, p):
    bsz, t_len, _ = qkv.shape
    dh = DN_HEAD_DIM
    per_b = lambda *shape: pl.BlockSpec((1,) + shape, lambda b: (b,) + (0,) * len(shape))
    const = lambda shape: pl.BlockSpec(shape, lambda b: (0,) * len(shape), pipeline_mode=pl.Buffered(1))
    lanes_row = const((1, V7X_LANES))
    return pl.pallas_call(
        _dn_sample_kernel,
        grid=(bsz,),
        in_specs=[lanes_row, lanes_row, per_b(t_len, 3 * DN_WIDTH), per_b(CONV_WIDTH - 1, 3 * DN_WIDTH), per_b(t_len, DN_WIDTH),
                  per_b(t_len, V7X_LANES), per_b(DN_HEADS, dh, dh), const((CONV_WIDTH, 3 * DN_WIDTH)), const((1, dh))],
        out_specs=[per_b(t_len, DN_WIDTH), per_b(CONV_WIDTH - 1, 3 * DN_WIDTH), per_b(DN_HEADS, dh, dh)],
        out_shape=[jax.ShapeDtypeStruct((bsz, t_len, DN_WIDTH), F32),
                   jax.ShapeDtypeStruct((bsz, CONV_WIDTH - 1, 3 * DN_WIDTH), F32),
                   jax.ShapeDtypeStruct((bsz, DN_HEADS, dh, dh), F32)],
        compiler_params=_params("arbitrary"),
        name="dn_sample",
    )(p["dn_A_log"], p["dn_dt_bias"], qkv, buf, z, ba, s0, p["dn_conv_w"], p["dn_o_gain"])


def _block_diag(w):
    n, c, d = w.shape
    eye = jnp.eye(n, dtype=w.dtype)
    return (eye[:, None, :, None] * w[:, :, None, :]).reshape(n * c, n * d)


def _prepare_weights(d):
    p = {}
    for name in ("w_ffn1_in", "w_ffn1_out", "w_ffn2_in", "w_ffn2_out"):
        p[name] = d[name].astype(BF16)
    p["w_in_even"] = d["w_in_even"][0].astype(BF16)
    p["w_out_even"] = d["w_out_even"][0].astype(BF16)
    p["ones_bd"] = _block_diag(jnp.ones((SB_HEADS, SB_HEAD_DIM, SB_HEAD_DIM), BF16))
    p["q_gain"] = jnp.tile(d["sb_q_gain"][0], SB_HEADS).reshape(1, SB_WIDTH)
    p["k_gain"] = jnp.tile(d["sb_k_gain"][0], SB_HEADS).reshape(1, SB_WIDTH)
    p["lru_conv_w"] = d["lru_conv_w"][0]
    p["lru_conv_b"] = d["lru_conv_b"][0].reshape(1, LRU_WIDTH)
    p["lru_wa"] = _block_diag(d["lru_w_a"][0]).astype(BF16)
    p["lru_wi"] = _block_diag(d["lru_w_i"][0]).astype(BF16)
    p["lru_ba"] = d["lru_b_a"][0].reshape(1, LRU_WIDTH)
    p["lru_bi"] = d["lru_b_i"][0].reshape(1, LRU_WIDTH)
    p["lru_lambda"] = d["lru_lambda"][0].reshape(1, LRU_WIDTH)
    w_odd = d["w_in_odd"][0]
    p["w_qkvz"] = w_odd[:, :4 * DN_WIDTH].astype(BF16)
    p["w_ba"] = jnp.pad(w_odd[:, 4 * DN_WIDTH:], ((0, 0), (0, V7X_LANES - 2 * DN_HEADS))).astype(BF16)
    p["w_out_odd"] = d["w_out_odd"][0].astype(BF16)
    p["dn_conv_w"] = d["dn_conv_w"][0]
    decay_lanes = lambda a: jnp.pad(a, (DN_HEADS, V7X_LANES - 2 * DN_HEADS)).reshape(1, V7X_LANES)
    p["dn_A_log"] = decay_lanes(d["dn_A_log"][0])
    p["dn_dt_bias"] = decay_lanes(d["dn_dt_bias"][0])
    p["dn_o_gain"] = d["dn_o_gain"][0].reshape(1, DN_HEAD_DIM)
    idx = jnp.arange(DN_CHUNK)
    p["dn_low"] = (idx[None, :] <= idx[:, None]).astype(BF16)
    later = (idx[:, None] > idx[None, :]).astype(F32)
    p["dn_uo"] = jnp.concatenate([later, jnp.zeros((DN_CHUNK, DN_CHUNK), F32), jnp.ones((DN_CHUNK, 2 * DN_CHUNK), F32)], axis=1)
    return p


def _even_layer_mixer(x, group, d, p):
    kind, bsz, t_len = group
    q, k, v, xr, xg = _even_in(x, d["norm_mix"][0], p["w_in_even"], p["ones_bd"], p["q_gain"], p["k_gain"])
    seq = lambda a: a.reshape(bsz, t_len, a.shape[-1])
    if kind == "prompt":
        attn = _sb_prompt(seq(q), seq(k), seq(v), d["sb_bias"][0], _tri_suffix(ATT_TILE))
        rec, tail, h_last = _lru_prompt(seq(xr), seq(xg), p)
        h_last = h_last[:, 0]
    else:
        bias_rows = jnp.broadcast_to(jnp.tile(d["sb_bias"][0], t_len)[:, None], (t_len * SB_HEADS, PAGE_SIZE))
        cache_k = jnp.transpose(d["cache_k"][0], (0, 2, 3, 1))
        cache_v = jnp.transpose(d["cache_v"][0], (0, 2, 3, 1))
        attn = _sb_sample(seq(q), seq(k), seq(v), cache_k, cache_v, d["page_table"], bias_rows, _tri_suffix(PAGE_SIZE))
        tm = lambda a: jnp.swapaxes(a, 0, 1)
        rec, tail, h_last = _lru_sample(tm(seq(xr)), tm(seq(xg)), tm(d["state_lru_conv"][0]), d["state_lru_h"][0], p)
        rec, tail = tm(rec), tm(tail)
    heads = lambda a: a.reshape(bsz, t_len, SB_HEADS, SB_HEAD_DIM)
    return [attn.reshape(-1, SB_WIDTH), rec.reshape(-1, LRU_WIDTH)], (heads(k), heads(v), tail, h_last)


def _odd_layer_mixer(x, group, d, p):
    kind, bsz, t_len = group
    qkv, z, ba = _odd_in(x, d["norm_mix"][1], p["w_qkvz"], p["w_ba"])
    seq = lambda a: a.reshape(bsz, t_len, a.shape[-1])
    if kind == "prompt":
        o, state = _dn_prompt(seq(qkv), seq(z), seq(ba), p)
        tail = seq(qkv)[:, t_len - (CONV_WIDTH - 1):, :]
    else:
        o, tail, state = _dn_sample(seq(qkv), d["state_dn_conv"][0], seq(z), seq(ba), d["state_dn_S"][0], p)
    return [o.reshape(-1, DN_WIDTH)], (tail, state)


def kernel(x_prompt, x_sample, cache_k, cache_v, state_lru_conv, state_lru_h, state_dn_conv, state_dn_S, page_table, norm_ffn1, w_ffn1_in, w_ffn1_out, norm_mix, norm_ffn2, w_ffn2_in, w_ffn2_out, w_in_even, sb_q_gain, sb_k_gain, sb_bias, lru_conv_w, lru_conv_b, lru_w_a, lru_b_a, lru_w_i, lru_b_i, lru_lambda, w_out_even, w_in_odd, dn_conv_w, dn_A_log, dn_dt_bias, dn_o_gain, w_out_odd):
    d = dict(locals())
    assert norm_ffn1.shape[0] == 2 and w_in_even.shape[0] == 1 and w_in_odd.shape[0] == 1, "two-layer trunk only"
    p = _prepare_weights(d)
    results = []
    for kind, x in (("prompt", x_prompt), ("sample", x_sample)):
        bsz, t_len, _ = x.shape
        group = (kind, bsz, t_len)
        x = x.reshape(bsz * t_len, D_MODEL)
        x = _ffn(x, norm_ffn1[0], p["w_ffn1_in"][0], p["w_ffn1_out"][0])
        mixed, even_state = _even_layer_mixer(x, group, d, p)
        x = _mix_ffn(mixed, p["w_out_even"], x, norm_ffn2[0], p["w_ffn2_in"][0], p["w_ffn2_out"][0])
        x = _ffn(x, norm_ffn1[1], p["w_ffn1_in"][1], p["w_ffn1_out"][1])
        mixed, odd_state = _odd_layer_mixer(x, group, d, p)
        x = _mix_ffn(mixed, p["w_out_odd"], x, norm_ffn2[1], p["w_ffn2_in"][1], p["w_ffn2_out"][1])
        results.append((x.reshape(bsz, t_len, D_MODEL),) + tuple(s[None] for s in even_state + odd_state))
    (yp, *prompt_state), (ys, *sample_state) = results
    return (yp, ys, *prompt_state, *sample_state)
```

```python
import functools
import math

import jax
import jax.numpy as jnp
from jax import lax
from jax.experimental import pallas as pl
from jax.experimental.pallas import tpu as pltpu

F32 = jnp.float32
BF16 = jnp.bfloat16

V7X_LANES = 128
V7X_SUBLANES = 8
V7X_VMEM_LIMIT_BYTES = 56 * 1024 * 1024

D_MODEL = 1024
D_FF = 2 * D_MODEL
SB_HEADS = 8
SB_HEAD_DIM = 64
SB_WIDTH = SB_HEADS * SB_HEAD_DIM
LRU_WIDTH = 512
LRU_BLOCKS = 8
LRU_C = 8.0
CONV_WIDTH = 4
DN_HEADS = 8
DN_HEAD_DIM = 128
DN_WIDTH = DN_HEADS * DN_HEAD_DIM
DN_CHUNK = 64
PAGE_SIZE = 128
EPS = 1e-6

TOKEN_TILE = 512
FF_CHUNK = 512
ATT_TILE = 256
LRU_TIME_TILE = 512


def _params(*sem):
    return pltpu.CompilerParams(dimension_semantics=sem, vmem_limit_bytes=V7X_VMEM_LIMIT_BYTES)


def _resident(shape):
    nd = len(shape)
    return pl.BlockSpec(shape, lambda *_: (0,) * nd, pipeline_mode=pl.Buffered(1))


def _rms_normalize(x, gain):
    ms = jnp.mean(x * x, axis=-1, keepdims=True)
    return x * lax.rsqrt(ms + EPS) * gain


LOG2E = 1.0 / math.log(2.0)


def _softplus(x):
    return jnp.maximum(x, 0.0) + jnp.log(1.0 + jnp.exp2(jnp.abs(x) * (-LOG2E)))


def _split2(x):
    hi = x.astype(BF16)
    lo = (x - hi.astype(F32)).astype(BF16)
    return hi, lo


def _split3(x):
    hi = x.astype(BF16)
    r1 = x - hi.astype(F32)
    mid = r1.astype(BF16)
    lo = (r1 - mid.astype(F32)).astype(BF16)
    return hi, mid, lo


def _dot(a, b):
    return jnp.dot(a, b, preferred_element_type=F32)


def _dot_nt(a, b):
    return lax.dot_general(a, b, (((1,), (1,)), ((), ())), preferred_element_type=F32)


def _dot_tn(a, b):
    return lax.dot_general(a, b, (((0,), (0,)), ((), ())), preferred_element_type=F32)


def _ffn_half_step(x, g_ref, win_ref, wout_ref, act_ref):
    xn = _rms_normalize(x, g_ref[...]).astype(BF16)
    for c in range(D_FF // FF_CHUNK):
        lo = c * FF_CHUNK
        gate = _dot(xn, win_ref[:, lo:lo + FF_CHUNK])
        up = _dot(xn, win_ref[:, D_FF + lo:D_FF + lo + FF_CHUNK])
        act_ref[:, lo:lo + FF_CHUNK] = (gate * jax.nn.sigmoid(gate) * up).astype(BF16)
    return x + 0.5 * _dot(act_ref[...], wout_ref[...])


def _ffn_kernel(x_ref, g_ref, win_ref, wout_ref, o_ref, act_ref):
    o_ref[...] = _ffn_half_step(x_ref[...], g_ref, win_ref, wout_ref, act_ref)


def _mix_ffn_kernel(*refs, n_parts):
    parts, (wmix_ref, x_ref, g_ref, win_ref, wout_ref, o_ref, act_ref) = refs[:n_parts], refs[n_parts:]
    x = x_ref[...]
    row = 0
    for p in parts:
        width = p.shape[-1]
        x = x + _dot(p[...].astype(BF16), wmix_ref[row:row + width, :])
        row += width
    o_ref[...] = _ffn_half_step(x, g_ref, win_ref, wout_ref, act_ref)


def _mix_ffn(parts, w_mix, x, gain, w_in, w_out):
    n = x.shape[0]
    tm = min(TOKEN_TILE, n)
    row = lambda i: (i, 0)
    return pl.pallas_call(
        functools.partial(_mix_ffn_kernel, n_parts=len(parts)),
        grid=(n // tm,),
        in_specs=[pl.BlockSpec((tm, p.shape[1]), row) for p in parts]
        + [_resident(w_mix.shape), pl.BlockSpec((tm, D_MODEL), row), _resident((1, D_MODEL)),
           _resident((D_MODEL, 2 * D_FF)), _resident((D_FF, D_MODEL))],
        out_specs=pl.BlockSpec((tm, D_MODEL), row),
        out_shape=jax.ShapeDtypeStruct((n, D_MODEL), F32),
        scratch_shapes=[pltpu.VMEM((tm, D_FF), BF16)],
        compiler_params=_params("arbitrary"),
        name="mix_ffn",
    )(*parts, w_mix, x, gain.reshape(1, D_MODEL), w_in, w_out)


def _ffn(x, gain, w_in, w_out):
    n = x.shape[0]
    tm = min(TOKEN_TILE, n)
    return pl.pallas_call(
        _ffn_kernel,
        grid=(n // tm,),
        in_specs=[
            pl.BlockSpec((tm, D_MODEL), lambda i: (i, 0)),
            _resident((1, D_MODEL)),
            _resident((D_MODEL, 2 * D_FF)),
            _resident((D_FF, D_MODEL)),
        ],
        out_specs=pl.BlockSpec((tm, D_MODEL), lambda i: (i, 0)),
        out_shape=jax.ShapeDtypeStruct((n, D_MODEL), F32),
        scratch_shapes=[pltpu.VMEM((tm, D_FF), BF16)],
        compiler_params=_params("arbitrary"),
        name="ffn",
    )(x, gain.reshape(1, D_MODEL), w_in, w_out)


def _head_rms(x, ones_bd, gain):
    hi, lo = _split2(x * x)
    ms = (_dot(hi, ones_bd) + _dot(lo, ones_bd)) * (1.0 / SB_HEAD_DIM)
    return x * lax.rsqrt(ms + EPS) * gain


def _even_in_kernel(x_ref, g_ref, w_ref, ones_ref, qg_ref, kg_ref,
                    q_ref, k_ref, v_ref, xr_ref, xg_ref, *kv_t_refs):
    xn = _rms_normalize(x_ref[...], g_ref[...]).astype(BF16)
    ones_bd = ones_ref[...]
    q = _dot(xn, w_ref[:, 0:SB_WIDTH])
    q_ref[...] = _head_rms(q, ones_bd, qg_ref[...])
    k = _head_rms(_dot(xn, w_ref[:, SB_WIDTH:2 * SB_WIDTH]), ones_bd, kg_ref[...])
    k_ref[...] = k
    v = _dot(xn, w_ref[:, 2 * SB_WIDTH:3 * SB_WIDTH])
    v_ref[...] = v
    if kv_t_refs:
        kv_t_refs[0][0] = k.T
        kv_t_refs[1][0] = v.T
    xr_ref[...] = _dot(xn, w_ref[:, 3 * SB_WIDTH:3 * SB_WIDTH + LRU_WIDTH])
    xg_ref[...] = _dot(xn, w_ref[:, 3 * SB_WIDTH + LRU_WIDTH:])


def _even_in(x, gain, w, ones_bd, q_gain, k_gain, seq_len=None):
    n = x.shape[0]
    tm = min(TOKEN_TILE, n)
    width = w.shape[1]
    row = lambda i: (i, 0)
    out = jax.ShapeDtypeStruct((n, SB_WIDTH), F32)
    extra_specs, extra_shapes = [], []
    if seq_len is not None:
        per_seq = seq_len // tm
        extra_specs = [pl.BlockSpec((1, SB_WIDTH, tm), lambda i: (i // per_seq, 0, i % per_seq))] * 2
        extra_shapes = [jax.ShapeDtypeStruct((n // seq_len, SB_WIDTH, seq_len), F32)] * 2
    return pl.pallas_call(
        _even_in_kernel,
        grid=(n // tm,),
        in_specs=[
            pl.BlockSpec((tm, D_MODEL), row),
            _resident((1, D_MODEL)),
            _resident((D_MODEL, width)),
            _resident((SB_WIDTH, SB_WIDTH)),
            _resident((1, SB_WIDTH)),
            _resident((1, SB_WIDTH)),
        ],
        out_specs=[pl.BlockSpec((tm, SB_WIDTH), row)] * 5 + extra_specs,
        out_shape=[out] * 5 + extra_shapes,
        compiler_params=_params("arbitrary"),
        name="even_in",
    )(x, gain.reshape(1, D_MODEL), w, ones_bd, q_gain, k_gain)


def _gelu_tanh(x):
    cdf = 0.5 * (1.0 + jnp.tanh(math.sqrt(2.0 / math.pi) * (x + 0.044715 * (x * x * x))))
    return x * cdf


def _lru_coeffs(xc, wa_ref, ba_ref, wi_ref, bi_ref, sp_lam):
    xb = xc.astype(BF16)
    r = jax.nn.sigmoid(_dot(xb, wa_ref[...]) + ba_ref[...])
    i = jax.nn.sigmoid(_dot(xb, wi_ref[...]) + bi_ref[...])
    log_a = -LRU_C * r * sp_lam
    a = jnp.exp(log_a)
    b = jnp.sqrt(-jnp.tanh(log_a) * (a * a + 1.0)) * (i * xc)
    return a, b


def _lru_prompt_kernel(xr_ref, xg_ref, cw_ref, cb_ref, wa_ref, ba_ref, wi_ref, bi_ref, lam_ref,
                       rec_ref, tail_ref, hlast_ref, xp_ref, a_ref, b_ref, h_ref, hc_ref):
    t = pl.program_id(1)
    tt = xr_ref.shape[1]
    pad = V7X_SUBLANES

    @pl.when(t == 0)
    def _():
        xp_ref[0:pad, :] = jnp.zeros((pad, LRU_WIDTH), F32)
        hc_ref[...] = jnp.zeros_like(hc_ref)

    @pl.when(t > 0)
    def _():
        xp_ref[0:pad, :] = xp_ref[tt:tt + pad, :]

    xp_ref[pad:pad + tt, :] = xr_ref[0]
    xc = cw_ref[0:1, :] * xp_ref[pad - 3:pad - 3 + tt, :]
    for j in range(1, CONV_WIDTH):
        xc = xc + cw_ref[j:j + 1, :] * xp_ref[pad - 3 + j:pad - 3 + j + tt, :]
    xc = xc + cb_ref[...]
    a, b = _lru_coeffs(xc, wa_ref, ba_ref, wi_ref, bi_ref, _softplus(-lam_ref[...]))
    groups = tt // V7X_SUBLANES
    ag = a.reshape(groups, V7X_SUBLANES, LRU_WIDTH)
    bg = b.reshape(groups, V7X_SUBLANES, LRU_WIDTH)
    row = lax.broadcasted_iota(jnp.int32, ag.shape, 1)
    for d in (1, 2, 4):
        keep = row >= d
        a_prev = jnp.where(keep, pltpu.roll(ag, d, axis=1), 1.0)
        b_prev = jnp.where(keep, pltpu.roll(bg, d, axis=1), 0.0)
        bg = ag * b_prev + bg
        ag = ag * a_prev
    a_ref[...] = ag[:, V7X_SUBLANES - 1, :]
    b_ref[...] = bg[:, V7X_SUBLANES - 1, :]

    def enter(j, h):
        h_ref[pl.ds(j, 1), :] = h
        return a_ref[pl.ds(j, 1), :] * h + b_ref[pl.ds(j, 1), :]

    h_end = lax.fori_loop(0, groups, enter, hc_ref[...])
    hc_ref[...] = h_end
    h_all = (ag * h_ref[...][:, None, :] + bg).reshape(tt, LRU_WIDTH)
    rec_ref[0] = h_all * _gelu_tanh(xg_ref[0])

    @pl.when(t == pl.num_programs(1) - 1)
    def _():
        tail_ref[0] = xp_ref[pad + tt - 3:pad + tt, :]
        hlast_ref[0] = h_end


def _lru_prompt(xr, xg, p):
    bsz, t_len, w = xr.shape
    tt = min(LRU_TIME_TILE, t_len)
    tile = pl.BlockSpec((1, tt, w), lambda b, t: (b, t, 0))
    per_b = lambda rows: pl.BlockSpec((1, rows, w), lambda b, t: (b, 0, 0))
    return pl.pallas_call(
        _lru_prompt_kernel,
        grid=(bsz, t_len // tt),
        in_specs=[tile, tile, _resident((CONV_WIDTH, w)), _resident((1, w)), _resident((w, w)),
                  _resident((1, w)), _resident((w, w)), _resident((1, w)), _resident((1, w))],
        out_specs=[tile, per_b(CONV_WIDTH - 1), per_b(1)],
        out_shape=[jax.ShapeDtypeStruct((bsz, t_len, w), F32),
                   jax.ShapeDtypeStruct((bsz, CONV_WIDTH - 1, w), F32),
                   jax.ShapeDtypeStruct((bsz, 1, w), F32)],
        scratch_shapes=[pltpu.VMEM((tt + 2 * V7X_SUBLANES, w), F32)]
        + [pltpu.VMEM((tt // V7X_SUBLANES, w), F32)] * 3 + [pltpu.VMEM((1, w), F32)],
        compiler_params=_params("arbitrary", "arbitrary"),
        name="lru_prompt",
    )(xr, xg, p["lru_conv_w"], p["lru_conv_b"], p["lru_wa"], p["lru_ba"], p["lru_wi"], p["lru_bi"],
      p["lru_lambda"])


def _lru_sample_kernel(xr_ref, xg_ref, buf_ref, h0_ref, cw_ref, cb_ref, wa_ref, ba_ref, wi_ref, bi_ref,
                       lam_ref, rec_ref, tail_ref, hlast_ref):
    t_len = xr_ref.shape[0]
    xp = [buf_ref[j] for j in range(CONV_WIDTH - 1)] + [xr_ref[j] for j in range(t_len)]
    sp_lam = _softplus(-lam_ref[...])
    h = h0_ref[...]
    for t in range(t_len):
        xc = cw_ref[0:1, :] * xp[t]
        for j in range(1, CONV_WIDTH):
            xc = xc + cw_ref[j:j + 1, :] * xp[t + j]
        xc = xc + cb_ref[...]
        a, b = _lru_coeffs(xc, wa_ref, ba_ref, wi_ref, bi_ref, sp_lam)
        h = a * h + b
        rec_ref[t] = h * _gelu_tanh(xg_ref[t])
    for j in range(CONV_WIDTH - 1):
        tail_ref[j] = xp[t_len + j]
    hlast_ref[...] = h


def _lru_sample(xr, xg, buf, h0, p):
    t_len, bsz, w = xr.shape
    full = lambda shape: pl.BlockSpec(shape, lambda i: (0,) * len(shape))
    return pl.pallas_call(
        _lru_sample_kernel,
        grid=(1,),
        in_specs=[full((t_len, bsz, w)), full((t_len, bsz, w)), full((CONV_WIDTH - 1, bsz, w)), full((bsz, w)),
                  full((CONV_WIDTH, w)), full((1, w)), full((w, w)), full((1, w)), full((w, w)), full((1, w)),
                  full((1, w))],
        out_specs=[full((t_len, bsz, w)), full((CONV_WIDTH - 1, bsz, w)), full((bsz, w))],
        out_shape=[jax.ShapeDtypeStruct((t_len, bsz, w), F32),
                   jax.ShapeDtypeStruct((CONV_WIDTH - 1, bsz, w), F32),
                   jax.ShapeDtypeStruct((bsz, w), F32)],
        compiler_params=_params("arbitrary"),
        name="lru_sample",
    )(xr, xg, buf, h0, p["lru_conv_w"], p["lru_conv_b"], p["lru_wa"], p["lru_ba"], p["lru_wi"], p["lru_bi"],
      p["lru_lambda"])


def _sb_chains(scores, masks, bias, tri, carry, one_dot):
    rows = scores[0].shape[0]
    zs, parts = [], []
    for s, m in zip(scores, masks):
        z = s + bias
        sp = _softplus(z)
        if m is not None:
            sp = jnp.where(m, sp, 0.0)
        zs.append(z)
        parts.append(jnp.concatenate(_split2(sp), axis=1))
    if one_dot:
        cum = _dot(jnp.concatenate(parts, axis=0), tri)
        cums = [cum[i * rows:(i + 1) * rows] for i in range(len(parts))]
    else:
        cums = [_dot(part, tri) for part in parts]
    ws = []
    for z, c, m in zip(zs, cums, masks):
        w = jnp.exp(z - c - carry)
        if m is not None:
            w = jnp.where(m, w, 0.0)
        ws.append(w.astype(BF16))
        carry = carry + c[:, 0:1]
    return carry, ws


def _sb_prompt_kernel(bias_ref, q_ref, k_ref, v_ref, tri_ref, o_ref):
    hp, qi = pl.program_id(1), pl.program_id(2)
    tq = q_ref.shape[1]
    kw = tq // 2
    q = q_ref[0] * (SB_HEAD_DIM ** -0.5)
    lane = lax.broadcasted_iota(jnp.int32, (tq, 2 * SB_HEAD_DIM), 1)
    q2 = jnp.concatenate([jnp.where(lane < SB_HEAD_DIM, q, 0.0), jnp.where(lane < SB_HEAD_DIM, 0.0, q)], axis=0)
    q2 = q2.astype(BF16)
    first_head = lax.broadcasted_iota(jnp.int32, (2 * tq, 1), 0) < tq
    bias = jnp.where(first_head, bias_ref[hp * 2], bias_ref[hp * 2 + 1])
    tri = tri_ref[...]

    def pair_step(j, masks, carry, acc):
        start = pl.multiple_of(j * tq, tq)
        newer, older = pl.ds(start + kw, kw), pl.ds(start, kw)
        scores = [_dot_nt(q2, k_ref[0, newer, :].astype(BF16)), _dot_nt(q2, k_ref[0, older, :].astype(BF16))]
        carry, ws = _sb_chains(scores, masks, bias, tri, carry, one_dot=False)
        acc = acc + _dot(ws[0], v_ref[0, newer, :].astype(BF16)) + _dot(ws[1], v_ref[0, older, :].astype(BF16))
        return carry, acc

    q_pos = lax.broadcasted_iota(jnp.int32, (2 * tq, kw), 0) & (tq - 1)
    k_pos = lax.broadcasted_iota(jnp.int32, (2 * tq, kw), 1)
    state = (jnp.zeros((2 * tq, 1), F32), jnp.zeros((2 * tq, 2 * SB_HEAD_DIM), F32))
    state = pair_step(qi, [k_pos + kw < q_pos, k_pos < q_pos], *state)
    state = lax.fori_loop(0, qi, lambda it, st: pair_step(qi - 1 - it, [None, None], *st), state)
    acc = state[1]
    o_ref[0] = jnp.where(lane < SB_HEAD_DIM, acc[:tq], acc[tq:])


def _sb_prompt(q, k, v, bias, tri):
    bsz, t_len, width = q.shape
    tq = 2 * ATT_TILE
    assert t_len % tq == 0 and tq & (tq - 1) == 0
    pair = 2 * SB_HEAD_DIM
    tile = pl.BlockSpec((1, tq, pair), lambda b, hp, qi: (b, qi, hp))
    seq = pl.BlockSpec((1, t_len, pair), lambda b, hp, qi: (b, 0, hp))
    return pl.pallas_call(
        _sb_prompt_kernel,
        grid=(bsz, width // pair, t_len // tq),
        in_specs=[pl.BlockSpec(memory_space=pltpu.SMEM), tile, seq, seq,
                  pl.BlockSpec((tq, ATT_TILE), lambda b, hp, qi: (0, 0), pipeline_mode=pl.Buffered(1))],
        out_specs=tile,
        out_shape=jax.ShapeDtypeStruct((bsz, t_len, width), F32),
        compiler_params=_params("arbitrary", "arbitrary", "arbitrary"),
        name="sb_prompt",
    )(bias, q, k, v, tri)


def _sb_sample_kernel(pt_ref, q_ref, kn_ref, vn_ref, bias_ref, tri_ref, *refs, n_pages):
    del pt_ref
    k_pages, v_pages, o_ref = refs[:n_pages], refs[n_pages:2 * n_pages], refs[2 * n_pages]
    t_len, width = q_ref.shape[1], q_ref.shape[2]
    rows = t_len * SB_HEADS
    q = q_ref[0] * (SB_HEAD_DIM ** -0.5)
    qb = jnp.broadcast_to(q[:, None, :], (t_len, SB_HEADS, width)).reshape(rows, width)
    row = lax.broadcasted_iota(jnp.int32, (rows, width), 0)
    lane = lax.broadcasted_iota(jnp.int32, (rows, width), 1)
    own_head = (row & (SB_HEADS - 1)) == (lane >> int(math.log2(SB_HEAD_DIM)))
    qbd = jnp.where(own_head, qb, 0.0).astype(BF16)
    page = lambda ref: ref[...].reshape(width, PAGE_SIZE).astype(BF16)

    pad = jnp.zeros((PAGE_SIZE - t_len, width), F32)
    k_new = jnp.concatenate([kn_ref[0], pad], axis=0).astype(BF16)
    v_new = jnp.concatenate([vn_ref[0], pad], axis=0).astype(BF16)
    key = lax.broadcasted_iota(jnp.int32, (rows, PAGE_SIZE), 1)
    tok = lax.broadcasted_iota(jnp.int32, (rows, PAGE_SIZE), 0) >> int(math.log2(SB_HEADS))
    order = list(reversed(range(n_pages)))
    scores = [_dot_nt(qbd, k_new)] + [_dot(qbd, page(k_pages[p])) for p in order]
    masks = [key < tok] + [None] * n_pages
    _, ws = _sb_chains(scores, masks, bias_ref[...], tri_ref[...], jnp.zeros((rows, 1), F32), one_dot=True)
    acc = _dot(ws[0], v_new)
    for w, p in zip(ws[1:], order):
        acc = acc + _dot_nt(w, page(v_pages[p]))
    acc = jnp.where(own_head, acc, 0.0)
    o_ref[0] = jnp.sum(acc.reshape(t_len, SB_HEADS, width), axis=1)


def _sb_sample(q, k_new, v_new, cache_k, cache_v, page_table, bias_rows, tri):
    bsz, t_len, width = q.shape
    n_pages = page_table.shape[1]
    rows = t_len * SB_HEADS
    tok = pl.BlockSpec((1, t_len, width), lambda b, pt: (b, 0, 0))
    const = lambda shape: pl.BlockSpec(shape, lambda b, pt: (0, 0), pipeline_mode=pl.Buffered(1))
    pages = [pl.BlockSpec((None, SB_HEADS, SB_HEAD_DIM, PAGE_SIZE), lambda b, pt, j=j: (pt[b, j], 0, 0, 0))
             for j in range(n_pages)]
    grid_spec = pltpu.PrefetchScalarGridSpec(
        num_scalar_prefetch=1,
        grid=(bsz,),
        in_specs=[tok, tok, tok, const((rows, PAGE_SIZE)), const((2 * PAGE_SIZE, PAGE_SIZE))] + pages + pages,
        out_specs=tok,
    )
    return pl.pallas_call(
        functools.partial(_sb_sample_kernel, n_pages=n_pages),
        grid_spec=grid_spec,
        out_shape=jax.ShapeDtypeStruct((bsz, t_len, width), F32),
        compiler_params=_params("arbitrary"),
        name="sb_sample",
    )(page_table, q, k_new, v_new, bias_rows, tri, *([cache_k] * n_pages), *([cache_v] * n_pages))


def _tri_suffix(n):
    idx = jnp.arange(n)
    tri = (idx[:, None] >= idx[None, :]).astype(BF16)
    return jnp.concatenate([tri, tri], axis=0)


ODD_COL_CHUNK = 1024
DN_TIME_TILE = 1024
DN_GROUP = 2


def _odd_in_kernel(x_ref, g_ref, w_ref, wba_ref, qkv_ref, z_ref, ba_ref):
    xn = _rms_normalize(x_ref[...], g_ref[...]).astype(BF16)
    for c in range(3 * DN_WIDTH // ODD_COL_CHUNK):
        lo = c * ODD_COL_CHUNK
        qkv_ref[:, lo:lo + ODD_COL_CHUNK] = _dot(xn, w_ref[:, lo:lo + ODD_COL_CHUNK])
    z_ref[...] = _dot(xn, w_ref[:, 3 * DN_WIDTH:])
    ba_ref[...] = _dot(xn, wba_ref[...])


def _odd_in(x, gain, w_qkvz, w_ba):
    n = x.shape[0]
    tm = min(TOKEN_TILE, n)
    row = lambda i: (i, 0)
    return pl.pallas_call(
        _odd_in_kernel,
        grid=(n // tm,),
        in_specs=[pl.BlockSpec((tm, D_MODEL), row), _resident((1, D_MODEL)),
                  _resident(w_qkvz.shape), _resident(w_ba.shape)],
        out_specs=[pl.BlockSpec((tm, 3 * DN_WIDTH), row), pl.BlockSpec((tm, DN_WIDTH), row),
                   pl.BlockSpec((tm, V7X_LANES), row)],
        out_shape=[jax.ShapeDtypeStruct((n, 3 * DN_WIDTH), F32), jax.ShapeDtypeStruct((n, DN_WIDTH), F32),
                   jax.ShapeDtypeStruct((n, V7X_LANES), F32)],
        compiler_params=_params("arbitrary"),
        name="odd_in",
    )(x, gain.reshape(1, D_MODEL), w_qkvz, w_ba)


def _l2_normalize(x):
    return x * lax.rsqrt(jnp.sum(x * x, axis=-1, keepdims=True) + EPS)


def _silu(x):
    return x * jax.nn.sigmoid(x)


def _head_column(x, lane_index):
    lane = lax.broadcasted_iota(jnp.int32, x.shape, 1)
    return jnp.sum(jnp.where(lane == lane_index, x, 0.0), axis=-1, keepdims=True)


def _dn_gates(ba, alog_row, dtb_row):
    beta = jax.nn.sigmoid(ba)
    g = -jnp.exp(alog_row) * _softplus(ba + dtb_row)
    return beta, g


def _dn_prompt_kernel(alog_ref, dtb_ref, q_ref, k_ref, v_ref, z_ref, ba_ref, cwq_ref, cwk_ref, cwv_ref,
                      gain_ref, low_ref, uo_ref, o_ref, s_ref, xq_ref, xk_ref, xv_ref, state_ref):
    grp, t = pl.program_id(1), pl.program_id(2)
    tt, gw = q_ref.shape[1], q_ref.shape[2]
    dh, cs, pad = DN_HEAD_DIM, DN_CHUNK, V7X_SUBLANES
    n_heads, n_chunks = gw // dh, tt // cs

    @pl.when(t == 0)
    def _():
        state_ref[...] = jnp.zeros_like(state_ref)

    def conv_silu(x_ref, xp_ref, cw_ref):
        @pl.when(t == 0)
        def _():
            xp_ref[0:pad, :] = jnp.zeros((pad, gw), F32)

        @pl.when(t > 0)
        def _():
            xp_ref[0:pad, :] = xp_ref[tt:tt + pad, :]

        xp_ref[pad:pad + tt, :] = x_ref[0]
        y = cw_ref[0:1, :] * xp_ref[pad - 3:pad - 3 + tt, :]
        for j in range(1, CONV_WIDTH):
            y = y + cw_ref[j:j + 1, :] * xp_ref[pad - 3 + j:pad - 3 + j + tt, :]
        return _silu(y)

    q_c, k_c, v_c = conv_silu(q_ref, xq_ref, cwq_ref), conv_silu(k_ref, xk_ref, cwk_ref), conv_silu(v_ref, xv_ref, cwv_ref)
    beta_all, g_all = _dn_gates(ba_ref[0], alog_ref[...], dtb_ref[...])
    per_head = {name: [] for name in ("q", "k", "v", "beta", "g")}
    for h in range(n_heads):
        lanes = slice(h * dh, (h + 1) * dh)
        head = grp * n_heads + h
        per_head["q"].append(_l2_normalize(q_c[:, lanes]) * (dh ** -0.5))
        per_head["k"].append(_l2_normalize(k_c[:, lanes]))
        per_head["v"].append(v_c[:, lanes])
        per_head["beta"].append(_head_column(beta_all, head))
        per_head["g"].append(_head_column(g_all, DN_HEADS + head))
    nb = n_heads * n_chunks
    stack = lambda name: jnp.concatenate(per_head[name], axis=0).reshape(nb, cs, -1)
    qc, kc, vc, beta, g = stack("q"), stack("k"), stack("v"), stack("beta"), stack("g")

    bmm = lambda a, b: lax.dot_general(a, b, (((2,), (1,)), ((0,), (0,))), preferred_element_type=F32)
    bmm_nt = lambda a, b: lax.dot_general(a, b, (((2,), (2,)), ((0,), (0,))), preferred_element_type=F32)
    low = jnp.broadcast_to(low_ref[...][None], (nb, cs, cs))
    row = lax.broadcasted_iota(jnp.int32, (nb, cs, cs), 1)
    col = lax.broadcasted_iota(jnp.int32, (nb, cs, cs), 2)
    parts = _split3(g * uo_ref[...][None])
    gfull = bmm(low, parts[0]) + bmm(low, parts[1]) + bmm(low, parts[2])
    decay = jnp.exp(gfull[:, :, 0:cs])
    g_col = gfull[:, :, 2 * cs:]
    e_g = jnp.exp(g_col)
    g_last = g_col[:, cs - 1:cs, :]
    kb = kc.astype(BF16)
    a_mat = jnp.where(col < row, bmm_nt(kb, kb) * decay, 0.0) * beta
    p_mat = jnp.where(col <= row, bmm_nt(qc.astype(BF16), kb) * decay, 0.0).astype(BF16)
    rhs = jnp.concatenate([vc * beta, kc * (beta * e_g)], axis=2)
    t_off, x = -a_mat, a_mat
    for _ in range(int(math.log2(cs)) - 1):
        xb = x.astype(BF16)
        x = bmm(xb, xb)
        t_off = t_off + x + bmm(t_off.astype(BF16), x.astype(BF16))
    sol = rhs + bmm(t_off.astype(BF16), rhs.astype(BF16))
    uv, wk = sol[:, :, 0:dh], sol[:, :, dh:].astype(BF16)
    qg = (qc * e_g).astype(BF16)
    kg = (kc * jnp.exp(g_last - g_col)).astype(BF16)
    g_end = jnp.exp(g_last)

    bmm_tn = lambda a, b: lax.dot_general(a, b, (((1,), (1,)), ((0,), (0,))), preferred_element_type=F32)
    mix = bmm_tn(kg, wk).astype(BF16)
    fresh = bmm_tn(kg, uv.astype(BF16))
    states = [state_ref[h] for h in range(n_heads)]
    entering = [[None] * n_chunks for _ in range(n_heads)]
    for c in range(n_chunks):
        for h in range(n_heads):
            i = h * n_chunks + c
            sb = states[h].astype(BF16)
            entering[h][c] = sb
            states[h] = g_end[i] * states[h] - _dot(mix[i], sb) + fresh[i]
    for h in range(n_heads):
        state_ref[h] = states[h]
    s_in = jnp.stack([entering[h][c] for h in range(n_heads) for c in range(n_chunks)], axis=0)
    ws = bmm(jnp.concatenate([wk, qg], axis=1), s_in)
    ub = (uv - ws[:, 0:cs, :]).astype(BF16)
    o = ws[:, cs:, :] + bmm(p_mat, ub)
    for h in range(n_heads):
        lanes = slice(h * dh, (h + 1) * dh)
        o_h = o[h * n_chunks:(h + 1) * n_chunks].reshape(tt, dh)
        o_ref[0, :, lanes] = _rms_normalize(o_h, gain_ref[...]) * _silu(z_ref[0, :, lanes])

    @pl.when(t == pl.num_programs(2) - 1)
    def _():
        for h in range(n_heads):
            s_ref[0, h] = states[h]


def _dn_prompt(qkv, z, ba, p):
    bsz, t_len, _ = qkv.shape
    tt = min(DN_TIME_TILE, t_len)
    dh = DN_HEAD_DIM
    gw = DN_GROUP * dh
    groups = DN_HEADS // DN_GROUP
    lane_blk = lambda off: pl.BlockSpec((1, tt, gw), lambda b, g, t: (b, t, off + g))
    cw_blk = lambda off: pl.BlockSpec((CONV_WIDTH, gw), lambda b, g, t: (0, off + g))
    const = lambda shape: pl.BlockSpec(shape, lambda b, g, t: (0,) * len(shape), pipeline_mode=pl.Buffered(1))
    lanes_row = const((1, V7X_LANES))
    return pl.pallas_call(
        _dn_prompt_kernel,
        grid=(bsz, groups, t_len // tt),
        in_specs=[lanes_row, lanes_row, lane_blk(0), lane_blk(groups), lane_blk(2 * groups), lane_blk(0),
                  pl.BlockSpec((1, tt, V7X_LANES), lambda b, g, t: (b, t, 0)),
                  cw_blk(0), cw_blk(groups), cw_blk(2 * groups),
                  const((1, dh)), const((DN_CHUNK, DN_CHUNK)), const((DN_CHUNK, 4 * DN_CHUNK))],
        out_specs=[lane_blk(0), pl.BlockSpec((1, DN_GROUP, dh, dh), lambda b, g, t: (b, g, 0, 0))],
        out_shape=[jax.ShapeDtypeStruct((bsz, t_len, DN_WIDTH), F32),
                   jax.ShapeDtypeStruct((bsz, DN_HEADS, dh, dh), F32)],
        scratch_shapes=[pltpu.VMEM((tt + 2 * V7X_SUBLANES, gw), F32)] * 3 + [pltpu.VMEM((DN_GROUP, dh, dh), F32)],
        compiler_params=_params("arbitrary", "arbitrary", "arbitrary"),
        name="dn_prompt",
    )(p["dn_A_log"], p["dn_dt_bias"], qkv, qkv, qkv, z, ba, p["dn_conv_w"], p["dn_conv_w"], p["dn_conv_w"],
      p["dn_o_gain"], p["dn_low"], p["dn_uo"])


def _dn_sample_kernel(alog_ref, dtb_ref, qkv_ref, buf_ref, z_ref, ba_ref, s0_ref, cw_ref, gain_ref,
                      o_ref, tail_ref, s_ref):
    t_len = qkv_ref.shape[1]
    dh = DN_HEAD_DIM
    xp = jnp.concatenate([buf_ref[0], qkv_ref[0]], axis=0)
    y = cw_ref[0:1, :] * xp[0:t_len]
    for j in range(1, CONV_WIDTH):
        y = y + cw_ref[j:j + 1, :] * xp[j:j + t_len]
    y = _silu(y)
    tail_ref[0] = xp[t_len:t_len + CONV_WIDTH - 1]
    beta_all, g_all = _dn_gates(ba_ref[0], alog_ref[...], dtb_ref[...])
    pad_rows = jnp.zeros((V7X_SUBLANES - t_len, dh), F32)
    ks, bases, grams = [], [], []
    for h in range(DN_HEADS):
        q = _l2_normalize(y[:, h * dh:(h + 1) * dh]) * (dh ** -0.5)
        k = _l2_normalize(y[:, DN_WIDTH + h * dh:DN_WIDTH + (h + 1) * dh])
        kq = jnp.concatenate([k, q], axis=0).astype(BF16)
        ks.append(k)
        bases.append(_dot(kq, s0_ref[0, h].astype(BF16)))
        grams.append(_dot_nt(kq, kq))
    updates = []
    for h in range(DN_HEADS):
        k, base, gram = ks[h], bases[h], grams[h]
        v = y[:, 2 * DN_WIDTH + h * dh:2 * DN_WIDTH + (h + 1) * dh]
        beta, g = _head_column(beta_all, h), _head_column(g_all, DN_HEADS + h)
        g_cum = [g[0:1, :]]
        for t in range(1, t_len):
            g_cum.append(g_cum[-1] + g[t:t + 1, :])
        us, outs = [], []
        for t in range(t_len):
            seen = jnp.exp(g_cum[t]) * base[t:t + 1, :]
            out = jnp.exp(g_cum[t]) * base[t_len + t:t_len + t + 1, :]
            for s in range(t):
                decay = jnp.exp(g_cum[t] - g_cum[s])
                seen = seen + (decay * gram[t:t + 1, s:s + 1]) * us[s]
                out = out + (decay * gram[t_len + t:t_len + t + 1, s:s + 1]) * us[s]
            u = beta[t:t + 1, :] * (v[t:t + 1, :] - seen)
            us.append(u)
            outs.append(out + gram[t_len + t:t_len + t + 1, t:t + 1] * u)
        g_end = g_cum[-1]
        kg = jnp.concatenate([k[t:t + 1, :] * jnp.exp(g_end - g_cum[t]) for t in range(t_len)] + [pad_rows], axis=0)
        u_all = jnp.concatenate(us + [pad_rows], axis=0)
        updates.append((jnp.exp(g_end), kg.astype(BF16), u_all.astype(BF16)))
        o = jnp.concatenate(outs, axis=0)
        o_ref[0, :, h * dh:(h + 1) * dh] = _rms_normalize(o, gain_ref[...]) * _silu(z_ref[0, :, h * dh:(h + 1) * dh])
    for h, (keep, kg, u_all) in enumerate(updates):
        s_ref[0, h] = keep * s0_ref[0, h] + _dot_tn(kg, u_all)


def _dn_sample(qkv, buf, z, ba, s0, p):
    bsz, t_len, _ = qkv.shape
    dh = DN_HEAD_DIM
    per_b = lambda *shape: pl.BlockSpec((1,) + shape, lambda b: (b,) + (0,) * len(shape))
    const = lambda shape: pl.BlockSpec(shape, lambda b: (0,) * len(shape), pipeline_mode=pl.Buffered(1))
    lanes_row = const((1, V7X_LANES))
    return pl.pallas_call(
        _dn_sample_kernel,
        grid=(bsz,),
        in_specs=[lanes_row, lanes_row, per_b(t_len, 3 * DN_WIDTH), per_b(CONV_WIDTH - 1, 3 * DN_WIDTH), per_b(t_len, DN_WIDTH),
                  per_b(t_len, V7X_LANES), per_b(DN_HEADS, dh, dh), const((CONV_WIDTH, 3 * DN_WIDTH)), const((1, dh))],
        out_specs=[per_b(t_len, DN_WIDTH), per_b(CONV_WIDTH - 1, 3 * DN_WIDTH), per_b(DN_HEADS, dh, dh)],
        out_shape=[jax.ShapeDtypeStruct((bsz, t_len, DN_WIDTH), F32),
                   jax.ShapeDtypeStruct((bsz, CONV_WIDTH - 1, 3 * DN_WIDTH), F32),
                   jax.ShapeDtypeStruct((bsz, DN_HEADS, dh, dh), F32)],
        compiler_params=_params("arbitrary"),
        name="dn_sample",
    )(p["dn_A_log"], p["dn_dt_bias"], qkv, buf, z, ba, s0, p["dn_conv_w"], p["dn_o_gain"])


def _block_diag(w):
    n, c, d = w.shape
    eye = jnp.eye(n, dtype=w.dtype)
    return (eye[:, None, :, None] * w[:, :, None, :]).reshape(n * c, n * d)


def _prepare_weights(d):
    p = {}
    for name in ("w_ffn1_in", "w_ffn1_out", "w_ffn2_in", "w_ffn2_out"):
        p[name] = d[name].astype(BF16)
    p["w_in_even"] = d["w_in_even"][0].astype(BF16)
    p["w_out_even"] = d["w_out_even"][0].astype(BF16)
    p["ones_bd"] = _block_diag(jnp.ones((SB_HEADS, SB_HEAD_DIM, SB_HEAD_DIM), BF16))
    p["q_gain"] = jnp.tile(d["sb_q_gain"][0], SB_HEADS).reshape(1, SB_WIDTH)
    p["k_gain"] = jnp.tile(d["sb_k_gain"][0], SB_HEADS).reshape(1, SB_WIDTH)
    p["lru_conv_w"] = d["lru_conv_w"][0]
    p["lru_conv_b"] = d["lru_conv_b"][0].reshape(1, LRU_WIDTH)
    p["lru_wa"] = _block_diag(d["lru_w_a"][0]).astype(BF16)
    p["lru_wi"] = _block_diag(d["lru_w_i"][0]).astype(BF16)
    p["lru_ba"] = d["lru_b_a"][0].reshape(1, LRU_WIDTH)
    p["lru_bi"] = d["lru_b_i"][0].reshape(1, LRU_WIDTH)
    p["lru_lambda"] = d["lru_lambda"][0].reshape(1, LRU_WIDTH)
    w_odd = d["w_in_odd"][0]
    p["w_qkvz"] = w_odd[:, :4 * DN_WIDTH].astype(BF16)
    p["w_ba"] = jnp.pad(w_odd[:, 4 * DN_WIDTH:], ((0, 0), (0, V7X_LANES - 2 * DN_HEADS))).astype(BF16)
    p["w_out_odd"] = d["w_out_odd"][0].astype(BF16)
    p["dn_conv_w"] = d["dn_conv_w"][0]
    decay_lanes = lambda a: jnp.pad(a, (DN_HEADS, V7X_LANES - 2 * DN_HEADS)).reshape(1, V7X_LANES)
    p["dn_A_log"] = decay_lanes(d["dn_A_log"][0])
    p["dn_dt_bias"] = decay_lanes(d["dn_dt_bias"][0])
    p["dn_o_gain"] = d["dn_o_gain"][0].reshape(1, DN_HEAD_DIM)
    idx = jnp.arange(DN_CHUNK)
    p["dn_low"] = (idx[None, :] <= idx[:, None]).astype(BF16)
    later = (idx[:, None] > idx[None, :]).astype(F32)
    p["dn_uo"] = jnp.concatenate([later, jnp.zeros((DN_CHUNK, DN_CHUNK), F32), jnp.ones((DN_CHUNK, 2 * DN_CHUNK), F32)], axis=1)
    return p


def _even_layer_mixer(x, group, d, p):
    kind, bsz, t_len = group
    outs = _even_in(x, d["norm_mix"][0], p["w_in_even"], p["ones_bd"], p["q_gain"], p["k_gain"],
                    seq_len=t_len if kind == "prompt" else None)
    q, k, v, xr, xg = outs[:5]
    seq = lambda a: a.reshape(bsz, t_len, a.shape[-1])
    heads = lambda a: a.reshape(bsz, t_len, SB_HEADS, SB_HEAD_DIM)
    k_out, v_out = heads(k), heads(v)
    if kind == "prompt":
        from_t = lambda a: jnp.transpose(a.reshape(bsz, SB_HEADS, SB_HEAD_DIM, t_len), (0, 3, 1, 2))
        k_out, v_out = from_t(outs[5]), from_t(outs[6])
        attn = _sb_prompt(seq(q), seq(k), seq(v), d["sb_bias"][0], _tri_suffix(ATT_TILE))
        rec, tail, h_last = _lru_prompt(seq(xr), seq(xg), p)
        h_last = h_last[:, 0]
    else:
        bias_rows = jnp.broadcast_to(jnp.tile(d["sb_bias"][0], t_len)[:, None], (t_len * SB_HEADS, PAGE_SIZE))
        cache_k = jnp.transpose(d["cache_k"][0], (0, 2, 3, 1))
        cache_v = jnp.transpose(d["cache_v"][0], (0, 2, 3, 1))
        attn = _sb_sample(seq(q), seq(k), seq(v), cache_k, cache_v, d["page_table"], bias_rows, _tri_suffix(PAGE_SIZE))
        tm = lambda a: jnp.swapaxes(a, 0, 1)
        rec, tail, h_last = _lru_sample(tm(seq(xr)), tm(seq(xg)), tm(d["state_lru_conv"][0]), d["state_lru_h"][0], p)
        rec, tail = tm(rec), tm(tail)
    return [attn.reshape(-1, SB_WIDTH), rec.reshape(-1, LRU_WIDTH)], (k_out, v_out, tail, h_last)


def _odd_layer_mixer(x, group, d, p):
    kind, bsz, t_len = group
    qkv, z, ba = _odd_in(x, d["norm_mix"][1], p["w_qkvz"], p["w_ba"])
    seq = lambda a: a.reshape(bsz, t_len, a.shape[-1])
    if kind == "prompt":
        o, state = _dn_prompt(seq(qkv), seq(z), seq(ba), p)
        tail = seq(qkv)[:, t_len - (CONV_WIDTH - 1):, :]
    else:
        o, tail, state = _dn_sample(seq(qkv), d["state_dn_conv"][0], seq(z), seq(ba), d["state_dn_S"][0], p)
    return [o.reshape(-1, DN_WIDTH)], (tail, state)


def kernel(x_prompt, x_sample, cache_k, cache_v, state_lru_conv, state_lru_h, state_dn_conv, state_dn_S, page_table, norm_ffn1, w_ffn1_in, w_ffn1_out, norm_mix, norm_ffn2, w_ffn2_in, w_ffn2_out, w_in_even, sb_q_gain, sb_k_gain, sb_bias, lru_conv_w, lru_conv_b, lru_w_a, lru_b_a, lru_w_i, lru_b_i, lru_lambda, w_out_even, w_in_odd, dn_conv_w, dn_A_log, dn_dt_bias, dn_o_gain, w_out_odd):
    d = dict(locals())
    assert norm_ffn1.shape[0] == 2 and w_in_even.shape[0] == 1 and w_in_odd.shape[0] == 1, "two-layer trunk only"
    p = _prepare_weights(d)
    results = []
    for kind, x in (("prompt", x_prompt), ("sample", x_sample)):
        bsz, t_len, _ = x.shape
        group = (kind, bsz, t_len)
        x = x.reshape(bsz * t_len, D_MODEL)
        x = _ffn(x, norm_ffn1[0], p["w_ffn1_in"][0], p["w_ffn1_out"][0])
        mixed, even_state = _even_layer_mixer(x, group, d, p)
        x = _mix_ffn(mixed, p["w_out_even"], x, norm_ffn2[0], p["w_ffn2_in"][0], p["w_ffn2_out"][0])
        x = _ffn(x, norm_ffn1[1], p["w_ffn1_in"][1], p["w_ffn1_out"][1])
        mixed, odd_state = _odd_layer_mixer(x, group, d, p)
        x = _mix_ffn(mixed, p["w_out_odd"], x, norm_ffn2[1], p["w_ffn2_in"][1], p["w_ffn2_out"][1])
        results.append((x.reshape(bsz, t_len, D_MODEL),) + tuple(s[None] for s in even_state + odd_state))
    (yp, *prompt_state), (ys, *sample_state) = results
    return (yp, ys, *prompt_state, *sample_state)
```
